```python
import math
import jax
import jax.numpy as jnp
from jax import lax
import numpy as np

D_MODEL = 2048
BATCH = 2
SEQ = 16384
DEPTH = 4

GRID_W = 64
CTX_LEN = 256
N_MIXERS = 3
HEAD_DIM = 128
N_HEADS = D_MODEL // HEAD_DIM
N_KV_HEADS = 4
GQA_GROUP = N_HEADS // N_KV_HEADS
Q_BLOCK = 128
ROPE_THETA = 10000.0
D_FF = -(-8 * D_MODEL // (3 * 256)) * 256
HYENA_EMB = 33
HYENA_ORDER = 64
HYENA_FAST_DECAY = 0.3
HYENA_SLOW_DECAY = 1.5
HYENA_TARGET = 1e-2
N_MOD = 6
NORM_EPS = 1e-6
F32 = jnp.float32

kernel_name = 'hybrid_hyena_shortconv_gqa_dit_trunk'


def rmsnorm(x, g):
    xf = x.astype(F32)
    y = xf * lax.rsqrt(jnp.mean(xf * xf, axis=-1, keepdims=True) + NORM_EPS)
    return (y * g.astype(F32)).astype(x.dtype)


def modulate(h, shift, scale):
    return h * (1 + scale) + shift


def adaln(cond, w_mod, b_mod):
    m = jax.nn.silu(cond) @ w_mod + b_mod
    return jnp.split(m[..., None, :], N_MOD, axis=-1)


def dwconv3(u, w):
    up = jnp.pad(u, ((0, 0), (1, 1), (0, 0)))
    return up[:, :-2] * w[0] + up[:, 1:-1] * w[1] + up[:, 2:] * w[2]


def swiglu(h, w_gate, w_up, w_down):
    return (jax.nn.silu(h @ w_gate) * (h @ w_up)) @ w_down


def hyena_filters(L, w1, b1, w2, b2, w3, b3, w4, freq):
    t = jnp.linspace(0.0, 1.0, L, dtype=F32)[:, None]
    bands = (HYENA_EMB - 1) // 2
    omega = 2.0 * math.pi * jnp.arange(L, dtype=F32)[:, None] / L
    fb = jnp.linspace(1e-4, bands - 1, bands, dtype=F32)[None, :]
    z = jnp.concatenate([t, jnp.cos(fb * omega), -jnp.sin(fb * omega)], axis=-1)
    fr = freq.astype(F32)
    hdn = jnp.sin(fr * (z @ w1.astype(F32) + b1.astype(F32)))
    hdn = jnp.sin(fr * (hdn @ w2.astype(F32) + b2.astype(F32)))
    hdn = jnp.sin(fr * (hdn @ w3.astype(F32) + b3.astype(F32)))
    h = hdn @ w4.astype(F32)
    max_decay = math.log(HYENA_TARGET) / HYENA_FAST_DECAY
    min_decay = math.log(HYENA_TARGET) / HYENA_SLOW_DECAY
    deltas = jnp.linspace(min_decay, max_decay, D_MODEL, dtype=F32)
    decay = jnp.exp(-t * jnp.abs(deltas)[None, :])
    h = h.reshape(L, 2, D_MODEL) * decay[:, None, :]
    return h[:, 0], h[:, 1]


def bidir_longconv(u, k_fwd, k_bwd):
    _, L, C = u.shape
    n = 2 * L
    k = jnp.concatenate([k_fwd, jnp.zeros((1, C), F32), k_bwd[:0:-1]], axis=0)
    U = jnp.fft.rfft(u.astype(F32), n=n, axis=1)
    K = jnp.fft.rfft(k, n=n, axis=0)
    y = jnp.fft.irfft(U * K[None], n=n, axis=1)[:, :L]
    return y.astype(u.dtype)


def hyena_mixer(h, w_in, b_in, short_w, short_b, f_w1, f_b1, f_w2, f_b2, f_w3, f_b3,
                f_w4, f_freq, skip_d, w_out, b_out):
    L = h.shape[1]
    u = dwconv3(h @ w_in + b_in, short_w) + short_b
    x0, x1, v = jnp.split(u, 3, axis=-1)
    k_fwd, k_bwd = hyena_filters(L, f_w1, f_b1, f_w2, f_b2, f_w3, f_b3, f_w4, f_freq)
    v = v * x1
    v = bidir_longconv(v, k_fwd, k_bwd) + v * skip_d
    return (v * x0) @ w_out + b_out


def shortconv_mixer(h, w_in, conv_w, w_out):
    gb, gc, xv = jnp.split(h @ w_in, 3, axis=-1)
    return (gb * dwconv3(gc * xv, conv_w)) @ w_out


def axial_rope_angles(L):
    rows = L // GRID_W
    d_axis = HEAD_DIM // 2
    inv_freq = ROPE_THETA ** (-jnp.arange(0, d_axis, 2, dtype=F32) / d_axis)
    ang_r = jnp.arange(rows, dtype=F32)[:, None] * inv_freq
    ang_c = jnp.arange(GRID_W, dtype=F32)[:, None] * inv_freq
    half = d_axis // 2
    ang = jnp.concatenate([jnp.broadcast_to(ang_r[:, None, :], (rows, GRID_W, half)),
                           jnp.broadcast_to(ang_c[None, :, :], (rows, GRID_W, half))], axis=-1)
    ang = ang.reshape(rows * GRID_W, d_axis)
    return jnp.cos(ang), jnp.sin(ang)


def apply_rope(x, cos, sin):
    xf = x.astype(F32).reshape(*x.shape[:-1], HEAD_DIM // 2, 2)
    x1, x2 = xf[..., 0], xf[..., 1]
    cs, sn = cos[None, :, None, :], sin[None, :, None, :]
    out = jnp.stack([x1 * cs - x2 * sn, x1 * sn + x2 * cs], axis=-1).reshape(x.shape)
    return out.astype(x.dtype)


def project_q(h, w_q, q_g):
    b, L, _ = h.shape
    return rmsnorm((h @ w_q).reshape(b, L, N_HEADS, HEAD_DIM), q_g)


def project_kv(h, w_kv, k_g):
    b, L, _ = h.shape
    k, v = jnp.split((h @ w_kv).reshape(b, L, 2 * N_KV_HEADS, HEAD_DIM), 2, axis=2)
    return rmsnorm(k, k_g), v


def gqa_attend(q, k, v):
    s = jnp.einsum('bqkgd,bskd->bkgqs', q, k).astype(F32) * (HEAD_DIM ** -0.5)
    p = jax.nn.softmax(s, axis=-1).astype(v.dtype)
    return jnp.einsum('bkgqs,bskd->bqkgd', p, v)


def attention_mixer(a, ac, w_qkv, q_g, k_g, w_o, ctx_out):
    b, L, _ = a.shape
    w_q, w_kv = w_qkv[:, :N_HEADS * HEAD_DIM], w_qkv[:, N_HEADS * HEAD_DIM:]
    cos, sin = axial_rope_angles(L)
    q = apply_rope(project_q(a, w_q, q_g), cos, sin)
    k, v = project_kv(a, w_kv, k_g)
    k = apply_rope(k, cos, sin)
    kc, vc = project_kv(ac, w_kv, k_g)
    k_all = jnp.concatenate([k, kc], axis=1)
    v_all = jnp.concatenate([v, vc], axis=1)
    n_blk = L // Q_BLOCK
    qb = q.reshape(b, n_blk, Q_BLOCK, N_KV_HEADS, GQA_GROUP, HEAD_DIM).swapaxes(0, 1)
    o = lax.map(lambda q_blk: gqa_attend(q_blk, k_all, v_all), qb)
    y = o.swapaxes(0, 1).reshape(b, L, N_HEADS * HEAD_DIM) @ w_o
    if not ctx_out:
        return y, None
    lc = ac.shape[1]
    qc = project_q(ac, w_q, q_g).reshape(b, lc, N_KV_HEADS, GQA_GROUP, HEAD_DIM)
    yc = gqa_attend(qc, kc, vc).reshape(b, lc, N_HEADS * HEAD_DIM) @ w_o
    return y, yc


def setup_inputs(seed: int = 0) -> dict:
    key = jax.random.key(seed)
    n_a = len(range(0, DEPTH, N_MIXERS))
    n_b = len(range(1, DEPTH, N_MIXERS))
    n_c = len(range(2, DEPTH, N_MIXERS))
    keys = iter(jax.random.split(key, 40))

    def nrm(shape, scale):
        return jax.random.normal(next(keys), shape, F32) * scale

    d, f = D_MODEL, D_FF
    qkv_cols = (N_HEADS + 2 * N_KV_HEADS) * HEAD_DIM
    return {
        'x': nrm((BATCH, SEQ, d), 1.0),
        'c': nrm((BATCH, d), 1.0),
        'ctx': nrm((BATCH, CTX_LEN, d), 1.0),
        'c_ctx': nrm((d,), 1.0),
        'norm1_g': 1.0 + nrm((DEPTH, d), 0.02),
        'norm2_g': 1.0 + nrm((DEPTH, d), 0.02),
        'w_mod': nrm((DEPTH, d, N_MOD * d), 0.5 * d ** -0.5),
        'b_mod': nrm((DEPTH, N_MOD * d), 0.02),
        'ffn_w_gate': nrm((DEPTH, d, f), d ** -0.5),
        'ffn_w_up': nrm((DEPTH, d, f), d ** -0.5),
        'ffn_w_down': nrm((DEPTH, f, d), f ** -0.5),
        'hy_w_in': nrm((n_a, d, 3 * d), d ** -0.5),
        'hy_b_in': nrm((n_a, 3 * d), 0.02),
        'hy_short_w': nrm((n_a, 3, 3 * d), 3 ** -0.5),
        'hy_short_b': nrm((n_a, 3 * d), 0.02),
        'hy_f_w1': nrm((n_a, HYENA_EMB, HYENA_ORDER), HYENA_EMB ** -0.5),
        'hy_f_b1': nrm((n_a, HYENA_ORDER), 0.1),
        'hy_f_w2': nrm((n_a, HYENA_ORDER, HYENA_ORDER), HYENA_ORDER ** -0.5),
        'hy_f_b2': nrm((n_a, HYENA_ORDER), 0.1),
        'hy_f_w3': nrm((n_a, HYENA_ORDER, HYENA_ORDER), HYENA_ORDER ** -0.5),
        'hy_f_b3': nrm((n_a, HYENA_ORDER), 0.1),
        'hy_f_w4': nrm((n_a, HYENA_ORDER, 2 * d), 0.01),
        'hy_f_freq': 1.0 + nrm((n_a, HYENA_ORDER), 0.02),
        'hy_skip': nrm((n_a, d), 1.0),
        'hy_w_out': nrm((n_a, d, d), d ** -0.5),
        'hy_b_out': nrm((n_a, d), 0.02),
        'sc_w_in': nrm((n_b, d, 3 * d), d ** -0.5),
        'sc_conv_w': nrm((n_b, 3, d), 3 ** -0.5),
        'sc_w_out': nrm((n_b, d, d), d ** -0.5),
        'at_w_qkv': nrm((n_c, d, qkv_cols), d ** -0.5),
        'at_q_g': 1.0 + nrm((n_c, HEAD_DIM), 0.02),
        'at_k_g': 1.0 + nrm((n_c, HEAD_DIM), 0.02),
        'at_w_o': nrm((n_c, d, d), d ** -0.5),
        'final_g': 1.0 + nrm((d,), 0.02),
    }


def reference(x, c, ctx, c_ctx, norm1_g, norm2_g, w_mod, b_mod, ffn_w_gate, ffn_w_up,
              ffn_w_down, hy_w_in, hy_b_in, hy_short_w, hy_short_b, hy_f_w1, hy_f_b1,
              hy_f_w2, hy_f_b2, hy_f_w3, hy_f_b3, hy_f_w4, hy_f_freq, hy_skip, hy_w_out,
              hy_b_out, sc_w_in, sc_conv_w, sc_w_out, at_w_qkv, at_q_g, at_k_g, at_w_o,
              final_g):
    attn_layers = [i for i in range(DEPTH) if i % N_MIXERS == 2]
    last_ctx_reader = attn_layers[-1] if attn_layers else -1
    xc = ctx
    for i in range(DEPTH):
        kind, j = i % N_MIXERS, i // N_MIXERS
        ctx_in = i <= last_ctx_reader
        ctx_out = i < last_ctx_reader
        sh1, sc1, g1, sh2, sc2, g2 = adaln(c, w_mod[i], b_mod[i])
        a = modulate(rmsnorm(x, norm1_g[i]), sh1, sc1)
        if ctx_in:
            csh1, csc1, cg1, csh2, csc2, cg2 = adaln(c_ctx, w_mod[i], b_mod[i])
            ac = modulate(rmsnorm(xc, norm1_g[i]), csh1, csc1)
        yc = None
        if kind == 0:
            hyp = (hy_w_in[j], hy_b_in[j], hy_short_w[j], hy_short_b[j], hy_f_w1[j], hy_f_b1[j],
                   hy_f_w2[j], hy_f_b2[j], hy_f_w3[j], hy_f_b3[j], hy_f_w4[j], hy_f_freq[j],
                   hy_skip[j], hy_w_out[j], hy_b_out[j])
            y = hyena_mixer(a, *hyp)
            if ctx_out:
                yc = hyena_mixer(ac, *hyp)
        elif kind == 1:
            scp = (sc_w_in[j], sc_conv_w[j], sc_w_out[j])
            y = shortconv_mixer(a, *scp)
            if ctx_out:
                yc = shortconv_mixer(ac, *scp)
        else:
            y, yc = attention_mixer(a, ac, at_w_qkv[j], at_q_g[j], at_k_g[j], at_w_o[j], ctx_out)
        ffn = (ffn_w_gate[i], ffn_w_up[i], ffn_w_down[i])
        x = x + g1 * y
        x = x + g2 * swiglu(modulate(rmsnorm(x, norm2_g[i]), sh2, sc2), *ffn)
        if ctx_out:
            xc = xc + cg1 * yc
            xc = xc + cg2 * swiglu(modulate(rmsnorm(xc, norm2_g[i]), csh2, csc2), *ffn)
    return rmsnorm(x, final_g)
```

```python
import functools
import math

import jax
import jax.numpy as jnp
from jax import lax
from jax.experimental import pallas as pl
from jax.experimental.pallas import tpu as pltpu

F32 = jnp.float32
BF16 = jnp.bfloat16
NORM_EPS = 1e-6
GRID_W = 64
ROPE_THETA = 10000.0
HYENA_FAST_DECAY = 0.3
HYENA_SLOW_DECAY = 1.5
HYENA_TARGET = 1e-2
N_MOD = 6
N_MIXERS = 3
FFT_N2 = 256
LANES = 128
SUBLANES = 8
HIGHEST = lax.Precision.HIGHEST


def _cp(sem, vmem_mb=None):
    kw = dict(dimension_semantics=sem)
    if vmem_mb is not None:
        kw["vmem_limit_bytes"] = vmem_mb << 20
    return pltpu.CompilerParams(**kw)


def _tile(n, pref):
    if n <= pref:
        return n
    t = pref
    while n % t:
        t //= 2
    return t


def _bdot(a, b):
    return jnp.dot(a, b, preferred_element_type=F32)


def _adaln_kernel(c_ref, w_ref, b_ref, o_ref):
    c = c_ref[...]
    s = (c * jax.nn.sigmoid(c)).astype(BF16)
    o_ref[...] = _bdot(s, w_ref[...].astype(BF16)) + b_ref[...]


def _adaln_all(cond8, w_mod, b_mod):
    depth, d, n = w_mod.shape
    tn = _tile(n, 1024)
    return pl.pallas_call(
        _adaln_kernel,
        grid=(depth, n // tn),
        in_specs=[
            pl.BlockSpec((SUBLANES, d), lambda l, j: (0, 0)),
            pl.BlockSpec((None, d, tn), lambda l, j: (l, 0, j)),
            pl.BlockSpec((None, 1, tn), lambda l, j: (l, 0, j)),
        ],
        out_specs=pl.BlockSpec((None, SUBLANES, tn), lambda l, j: (l, 0, j)),
        out_shape=jax.ShapeDtypeStruct((depth, SUBLANES, n), F32),
        compiler_params=_cp(("parallel", "parallel"), 40),
        name="adaln",
    )(cond8, w_mod, b_mod.reshape(depth, 1, n))


def _norm_mod(x, g, sh, sc):
    rs = lax.rsqrt(jnp.mean(x * x, axis=-1, keepdims=True) + NORM_EPS)
    return ((x * rs) * g) * (1.0 + sc) + sh


def _nm_kernel(x_ref, g_ref, sh_ref, sc_ref, w_ref, b_ref, o_ref, a_scr):
    @pl.when(pl.program_id(2) == 0)
    def _():
        a_scr[...] = _norm_mod(x_ref[...], g_ref[...], sh_ref[...], sc_ref[...]).astype(BF16)

    o_ref[...] = (_bdot(a_scr[...], w_ref[...]) + b_ref[...]).astype(o_ref.dtype)


def _norm_mod_matmul(x, g, sh, sc, w, bias, tm_pref=512, tn_pref=512):
    bsz, l, d = x.shape
    n = w.shape[1]
    tm, tn = _tile(l, tm_pref), _tile(n, tn_pref)
    return pl.pallas_call(
        _nm_kernel,
        grid=(bsz, l // tm, n // tn),
        in_specs=[
            pl.BlockSpec((None, tm, d), lambda b, i, j: (b, i, 0)),
            pl.BlockSpec((1, d), lambda b, i, j: (0, 0)),
            pl.BlockSpec((None, 1, d), lambda b, i, j: (b, 0, 0)),
            pl.BlockSpec((None, 1, d), lambda b, i, j: (b, 0, 0)),
            pl.BlockSpec((d, tn), lambda b, i, j: (0, j)),
            pl.BlockSpec((1, tn), lambda b, i, j: (0, j)),
        ],
        out_specs=pl.BlockSpec((None, tm, tn), lambda b, i, j: (b, i, j)),
        out_shape=jax.ShapeDtypeStruct((bsz, l, n), F32),
        scratch_shapes=[pltpu.VMEM((tm, d), BF16)],
        compiler_params=_cp(("parallel", "parallel", "arbitrary"), 40),
        name="norm_mod_matmul",
    )(x, g.reshape(1, d), sh, sc, w, bias.reshape(1, n))


def _ffn_kernel(x_ref, g_ref, sh_ref, sc_ref, gate_ref, wg_ref, wu_ref, wd_ref, o_ref, a_scr, acc_scr):
    f = pl.program_id(2)

    @pl.when(f == 0)
    def _():
        a_scr[...] = _norm_mod(x_ref[...], g_ref[...], sh_ref[...], sc_ref[...]).astype(BF16)
        acc_scr[...] = jnp.zeros_like(acc_scr)

    a = a_scr[...]
    hg = _bdot(a, wg_ref[...])
    hu = _bdot(a, wu_ref[...])
    h = (hg * jax.nn.sigmoid(hg)) * hu
    acc_scr[...] += _bdot(h.astype(BF16), wd_ref[...])

    @pl.when(f == pl.num_programs(2) - 1)
    def _():
        o_ref[...] = x_ref[...] + gate_ref[...] * acc_scr[...]


def _ffn(x, g, sh, sc, gate, wg, wu, wd, tm_pref=512, tf_pref=512):
    bsz, l, d = x.shape
    fdim = wg.shape[1]
    tm, tf = _tile(l, tm_pref), _tile(fdim, tf_pref)
    return pl.pallas_call(
        _ffn_kernel,
        grid=(bsz, l // tm, fdim // tf),
        in_specs=[
            pl.BlockSpec((None, tm, d), lambda b, i, f: (b, i, 0)),
            pl.BlockSpec((1, d), lambda b, i, f: (0, 0)),
            pl.BlockSpec((None, 1, d), lambda b, i, f: (b, 0, 0)),
            pl.BlockSpec((None, 1, d), lambda b, i, f: (b, 0, 0)),
            pl.BlockSpec((None, 1, d), lambda b, i, f: (b, 0, 0)),
            pl.BlockSpec((d, tf), lambda b, i, f: (0, f)),
            pl.BlockSpec((d, tf), lambda b, i, f: (0, f)),
            pl.BlockSpec((tf, d), lambda b, i, f: (f, 0)),
        ],
        out_specs=pl.BlockSpec((None, tm, d), lambda b, i, f: (b, i, 0)),
        out_shape=jax.ShapeDtypeStruct((bsz, l, d), F32),
        scratch_shapes=[pltpu.VMEM((tm, d), BF16), pltpu.VMEM((tm, d), F32)],
        compiler_params=_cp(("parallel", "parallel", "arbitrary"), 48),
        name="ffn",
    )(x, g.reshape(1, d), sh, sc, gate, wg, wu, wd)


def _out_kernel(y_ref, w_ref, b_ref, x_ref, gate_ref, o_ref):
    o_ref[...] = x_ref[...] + gate_ref[...] * (_bdot(y_ref[...].astype(BF16), w_ref[...]) + b_ref[...])


def _out_proj(y, w, bias, x, gate, tm_pref=512, tn_pref=1024):
    bsz, l, d = x.shape
    k = w.shape[0]
    tm, tn = _tile(l, tm_pref), _tile(d, tn_pref)
    return pl.pallas_call(
        _out_kernel,
        grid=(bsz, l // tm, d // tn),
        in_specs=[
            pl.BlockSpec((None, tm, k), lambda b, i, j: (b, i, 0)),
            pl.BlockSpec((k, tn), lambda b, i, j: (0, j)),
            pl.BlockSpec((1, tn), lambda b, i, j: (0, j)),
            pl.BlockSpec((None, tm, tn), lambda b, i, j: (b, i, j)),
            pl.BlockSpec((None, 1, tn), lambda b, i, j: (b, 0, j)),
        ],
        out_specs=pl.BlockSpec((None, tm, tn), lambda b, i, j: (b, i, j)),
        out_shape=jax.ShapeDtypeStruct((bsz, l, d), F32),
        compiler_params=_cp(("parallel", "parallel", "parallel"), 40),
        name="out_proj",
    )(y, w, bias.reshape(1, d), x, gate)


def _final_norm_kernel(x_ref, g_ref, o_ref):
    x = x_ref[...]
    o_ref[...] = (x * lax.rsqrt(jnp.mean(x * x, axis=-1, keepdims=True) + NORM_EPS)) * g_ref[...]


def _final_norm(x, g, tm_pref=512):
    bsz, l, d = x.shape
    tm = _tile(l, tm_pref)
    return pl.pallas_call(
        _final_norm_kernel,
        grid=(bsz, l // tm),
        in_specs=[pl.BlockSpec((None, tm, d), lambda b, i: (b, i, 0)), pl.BlockSpec((1, d), lambda b, i: (0, 0))],
        out_specs=pl.BlockSpec((None, tm, d), lambda b, i: (b, i, 0)),
        out_shape=jax.ShapeDtypeStruct((bsz, l, d), F32),
        compiler_params=_cp(("parallel", "parallel")),
        name="final_norm",
    )(x, g.reshape(1, d))


def _conv3(x, prev_row, next_row, w):
    tm = x.shape[0]
    row = lax.broadcasted_iota(jnp.int32, x.shape, 0)
    up = jnp.where(row == 0, prev_row, pltpu.roll(x, 1, 0))
    dn = jnp.where(row == tm - 1, next_row, pltpu.roll(x, tm - 1, 0))
    return up * w[0:1] + x * w[1:2] + dn * w[2:3]


def _halo_rows(p_ref, n_ref):
    i = pl.program_id(1)
    last = pl.num_programs(1) - 1
    prev_row = jnp.where(i == 0, 0.0, p_ref[SUBLANES - 1:SUBLANES, :])
    next_row = jnp.where(i == last, 0.0, n_ref[0:1, :])
    return prev_row, next_row


def _hy_gate_kernel(m0, p0, n0, m1, p1, n1, m2, p2, n2, w0, w1, w2, b0, b1, b2, vg_ref, x0_ref):
    def comp(m, p, n, w, b):
        pr, nx = _halo_rows(p, n)
        return _conv3(m[...], pr, nx, w[...]) + b[...]

    u0 = comp(m0, p0, n0, w0, b0)
    u1 = comp(m1, p1, n1, w1, b1)
    u2 = comp(m2, p2, n2, w2, b2)
    vg_ref[...] = u2 * u1
    x0_ref[...] = u0


def _sc_gate_kernel(m0, m1, p1, n1, m2, p2, n2, w_ref, o_ref):
    i = pl.program_id(1)
    last = pl.num_programs(1) - 1
    prev_row = jnp.where(i == 0, 0.0, p1[SUBLANES - 1:SUBLANES, :] * p2[SUBLANES - 1:SUBLANES, :])
    next_row = jnp.where(i == last, 0.0, n1[0:1, :] * n2[0:1, :])
    o_ref[...] = (m0[...] * _conv3(m1[...] * m2[...], prev_row, next_row, w_ref[...])).astype(o_ref.dtype)


def _halo_specs(tm, tc, l, col):
    r = tm // SUBLANES
    nblk = l // SUBLANES
    return [
        pl.BlockSpec((None, tm, tc), lambda b, i, j: (b, i, col + j)),
        pl.BlockSpec((None, SUBLANES, tc), lambda b, i, j: (b, jnp.maximum(i * r - 1, 0), col + j)),
        pl.BlockSpec((None, SUBLANES, tc), lambda b, i, j: (b, jnp.minimum((i + 1) * r, nblk - 1), col + j)),
    ]


def _hy_gate(z, short_w, short_b, tc_pref=1024):
    bsz, l, d3 = z.shape
    d = d3 // 3
    tm = _tile(l, 256)
    tc = _tile(d, tc_pref)
    nd = d // tc
    in_specs = []
    for comp in range(3):
        in_specs += _halo_specs(tm, tc, l, comp * nd)
    for comp in range(3):
        in_specs.append(pl.BlockSpec((3, tc), lambda b, i, j, c=comp: (0, c * nd + j)))
    for comp in range(3):
        in_specs.append(pl.BlockSpec((1, tc), lambda b, i, j, c=comp: (0, c * nd + j)))
    ospec = pl.BlockSpec((None, tm, tc), lambda b, i, j: (b, i, j))
    return pl.pallas_call(
        _hy_gate_kernel,
        grid=(bsz, l // tm, nd),
        in_specs=in_specs,
        out_specs=[ospec, ospec],
        out_shape=[jax.ShapeDtypeStruct((bsz, l, d), F32)] * 2,
        compiler_params=_cp(("parallel", "parallel", "parallel"), 40),
        name="hyena_gate",
    )(*([z] * 9), short_w, short_w, short_w, *([short_b.reshape(1, d3)] * 3))


def _sc_gate(z, conv_w, tc_pref=1024):
    bsz, l, d3 = z.shape
    d = d3 // 3
    tm = _tile(l, 256)
    tc = _tile(d, tc_pref)
    nd = d // tc
    in_specs = [_halo_specs(tm, tc, l, 0)[0]] + _halo_specs(tm, tc, l, nd) + _halo_specs(tm, tc, l, 2 * nd)
    in_specs.append(pl.BlockSpec((3, tc), lambda b, i, j: (0, j)))
    return pl.pallas_call(
        _sc_gate_kernel,
        grid=(bsz, l // tm, nd),
        in_specs=in_specs,
        out_specs=pl.BlockSpec((None, tm, tc), lambda b, i, j: (b, i, j)),
        out_shape=jax.ShapeDtypeStruct((bsz, l, d), BF16),
        compiler_params=_cp(("parallel", "parallel", "parallel"), 40),
        name="shortconv_gate",
    )(*([z] * 7), conv_w)


def _filter_features(l):
    emb = 33
    bands = (emb - 1) // 2
    t = jnp.linspace(0.0, 1.0, l, dtype=F32)[:, None]
    omega = 2.0 * math.pi * jnp.arange(l, dtype=F32)[:, None] / l
    fb = jnp.linspace(1e-4, bands - 1, bands, dtype=F32)[None, :]
    z = jnp.concatenate([t, jnp.cos(fb * omega), -jnp.sin(fb * omega)], axis=-1)
    z = jnp.pad(z, ((0, 0), (0, LANES - emb)))
    rows = jnp.arange(2 * l, dtype=jnp.int32)
    pos = jnp.where(rows < l, rows, 2 * l - rows)
    return z[jnp.clip(pos, 0, l - 1)]


def _filt_kernel(z_ref, w1, b1, w2, b2, w3, b3, fr, w4, dab, o_ref):
    z = z_ref[...]

    def hdot(a, b):
        return jnp.dot(a, b, precision=HIGHEST, preferred_element_type=F32)

    f = fr[...]
    h = jnp.sin(f * (hdot(z, w1[...]) + b1[...]))
    h = jnp.sin(f * (hdot(h, w2[...]) + b2[...]))
    h = jnp.sin(f * (hdot(h, w3[...]) + b3[...]))
    k = hdot(h, w4[...]) * jnp.exp(-hdot(z, dab[...]))
    row = lax.broadcasted_iota(jnp.int32, k.shape, 0)
    first_bwd = pl.program_id(0) * 2 == pl.num_programs(0)
    o_ref[...] = jnp.where((row == 0) & first_bwd, 0.0, k)


def _hyena_filter(l, p, d, tc_pref=1024):
    w1, b1, w2, b2, w3, b3, w4, freq = p
    order = w2.shape[0]
    zf = _filter_features(l)

    def pad2(a, rows, cols):
        return jnp.pad(a, ((0, rows - a.shape[0]), (0, cols - a.shape[1])))

    w1p = pad2(w1, LANES, LANES)
    w2p = pad2(w2, LANES, LANES)
    w3p = pad2(w3, LANES, LANES)
    b1p = pad2(b1.reshape(1, order), 1, LANES)
    b2p = pad2(b2.reshape(1, order), 1, LANES)
    b3p = pad2(b3.reshape(1, order), 1, LANES)
    frp = pad2(freq.reshape(1, order), 1, LANES)
    w4p = pad2(w4, LANES, 2 * d)
    max_decay = math.log(HYENA_TARGET) / HYENA_FAST_DECAY
    min_decay = math.log(HYENA_TARGET) / HYENA_SLOW_DECAY
    deltas = jnp.abs(jnp.linspace(min_decay, max_decay, d, dtype=F32))
    dab = pad2(deltas.reshape(1, d), LANES, d)
    tc = _tile(d, tc_pref)
    nd = d // tc
    r = _tile(l, 512)
    nb = 2 * l // r
    small = pl.BlockSpec((LANES, LANES), lambda i, j: (0, 0))
    vec = pl.BlockSpec((1, LANES), lambda i, j: (0, 0))
    bank = pl.BlockSpec((LANES, tc), lambda i, j: (0, (i * 2 // nb) * nd + j))
    return pl.pallas_call(
        _filt_kernel,
        grid=(nb, nd),
        in_specs=[pl.BlockSpec((r, LANES), lambda i, j: (i, 0)),
                  small, vec, small, vec, small, vec, vec, bank, pl.BlockSpec((LANES, tc), lambda i, j: (0, j))],
        out_specs=pl.BlockSpec((r, tc), lambda i, j: (i, j)),
        out_shape=jax.ShapeDtypeStruct((2 * l, d), F32),
        compiler_params=_cp(("parallel", "parallel")),
        name="hyena_filter",
    )(zf, w1p, b1p, w2p, b2p, w3p, b3p, frp, w4p, dab)


def _cs(m, n):
    ang = (m % n).astype(F32) * (2.0 * math.pi / n)
    return jnp.cos(ang), jnp.sin(ang)


def _blk(a, b, c, d):
    return jnp.concatenate([jnp.concatenate([a, b], axis=-1), jnp.concatenate([c, d], axis=-1)], axis=-2)


def _expand_tiles(g):
    n2, k, r = g.shape
    g4 = g.reshape(n2 // SUBLANES, SUBLANES, k, r)
    eye = jnp.eye(SUBLANES, dtype=g.dtype)
    out = jnp.einsum("tjkr,ji->tkirj", g4, eye)
    return out.reshape(n2 // SUBLANES, SUBLANES * k, SUBLANES * r)


def _fft_tables(l):
    n = 2 * l
    n1 = n // FFT_N2
    a_cnt = n1 // 2
    k1 = jnp.arange(n1, dtype=jnp.int32)
    b = jnp.arange(FFT_N2, dtype=jnp.int32)
    a_full = jnp.arange(n1, dtype=jnp.int32)
    pos = FFT_N2 * a_full[None, :] + b[:, None]
    c, s = _cs(k1[None, :, None] * pos[:, None, :], n)
    cd, sd = c[:, :, :a_cnt], s[:, :, :a_cnt]
    g_data = _expand_tiles(_blk(cd, sd, -sd, cd).astype(BF16))
    g_filt = _expand_tiles(jnp.concatenate([c, -s], axis=-2).astype(BF16))
    ct, st = jnp.swapaxes(cd, 1, 2), jnp.swapaxes(sd, 1, 2)
    g_fin = _expand_tiles(_blk(ct, -st, st, ct).astype(BF16))
    c2, s2 = _cs(b[:, None] * b[None, :], FFT_N2)
    f2_fwd = _blk(c2, s2, -s2, c2).astype(BF16)
    f2_inv = _blk(c2, -s2, s2, c2).astype(BF16)
    return g_data, g_filt, g_fin, f2_fwd, f2_inv


def _s1_kernel(x_ref, g_ref, o_ref):
    rows, sub, tc = x_ref.shape
    x = x_ref[...].reshape(rows * sub, tc).astype(BF16)
    o_ref[...] = _bdot(g_ref[...], x).reshape(o_ref.shape)


def _fft_stage1(x4, g, tc_pref=512):
    r, nt, sub, d = x4.shape
    m = g.shape[1] // sub
    tc = _tile(d, tc_pref)
    return pl.pallas_call(
        _s1_kernel,
        grid=(nt, d // tc),
        in_specs=[pl.BlockSpec((r, None, sub, tc), lambda t, j: (0, t, 0, j)),
                  pl.BlockSpec((None, sub * m, sub * r), lambda t, j: (t, 0, 0))],
        out_specs=pl.BlockSpec((m, None, sub, tc), lambda t, j: (0, t, 0, j)),
        out_shape=jax.ShapeDtypeStruct((m, nt, sub, d), F32),
        compiler_params=_cp(("parallel", "arbitrary"), 48),
        name="fft_stage1",
    )(x4, g)


def _fmid_kernel(y_ref, f_ref, o_ref, *, scale):
    y = jnp.concatenate([y_ref[0], y_ref[1]], axis=0).astype(BF16)
    o_ref[...] = _bdot(f_ref[...], y) * scale


def _fft_filter_spectrum(y1, f2_fwd, scale, tc_pref=1024):
    m, n2, d = y1.shape
    n1 = m // 2
    tc = _tile(d, tc_pref)
    y4 = y1.reshape(2, n1, n2, d)
    return pl.pallas_call(
        functools.partial(_fmid_kernel, scale=scale),
        grid=(n1, d // tc),
        in_specs=[pl.BlockSpec((2, None, n2, tc), lambda k, j: (0, k, 0, j)),
                  pl.BlockSpec((2 * n2, 2 * n2), lambda k, j: (0, 0))],
        out_specs=pl.BlockSpec((None, 2 * n2, tc), lambda k, j: (k, 0, j)),
        out_shape=jax.ShapeDtypeStruct((n1, 2 * n2, d), F32),
        compiler_params=_cp(("parallel", "parallel"), 40),
        name="fft_filter_stage2",
    )(y4, f2_fwd)


def _dmid_kernel(y_ref, k_ref, ff_ref, fi_ref, t_ref):
    n2 = y_ref.shape[1]
    y = jnp.concatenate([y_ref[0], y_ref[1]], axis=0).astype(BF16)
    s = _bdot(ff_ref[...], y)
    sr, si = s[:n2], s[n2:]
    kr, ki = k_ref[:n2, :], k_ref[n2:, :]
    p = jnp.concatenate([sr * kr - si * ki, sr * ki + si * kr], axis=0).astype(BF16)
    t = _bdot(fi_ref[...], p)
    t_ref[0] = t[:n2]
    t_ref[1] = t[n2:]


def _fft_data_mid(y1, kspec, f2_fwd, f2_inv, tc_pref=1024):
    m, n2, d = y1.shape
    n1 = m // 2
    tc = _tile(d, tc_pref)
    y4 = y1.reshape(2, n1, n2, d)
    tbl = pl.BlockSpec((2 * n2, 2 * n2), lambda k, j: (0, 0))
    yspec = pl.BlockSpec((2, None, n2, tc), lambda k, j: (0, k, 0, j))
    return pl.pallas_call(
        _dmid_kernel,
        grid=(n1, d // tc),
        in_specs=[yspec, pl.BlockSpec((None, 2 * n2, tc), lambda k, j: (k, 0, j)), tbl, tbl],
        out_specs=yspec,
        out_shape=jax.ShapeDtypeStruct((2, n1, n2, d), F32),
        compiler_params=_cp(("parallel", "parallel"), 48),
        name="fft_data_mid",
    )(y4, kspec, f2_fwd, f2_inv)


def _fin_kernel(t_ref, g_ref, vg_ref, x0_ref, skip_ref, o_ref):
    rows, sub, tc = t_ref.shape
    t = t_ref[...].reshape(rows * sub, tc).astype(BF16)
    conv = _bdot(g_ref[...], t).reshape(o_ref.shape)
    o_ref[...] = (conv + vg_ref[...] * skip_ref[...]) * x0_ref[...]


def _fft_final(t4, g_fin, vg, x0, skip, tc_pref=512):
    _, n1, n2, d = t4.shape
    bsz, l, _ = vg.shape
    nt = n2 // SUBLANES
    rows_in = 2 * n1
    rows_out = bsz * l // n2
    tc = _tile(d, tc_pref)
    vspec = pl.BlockSpec((rows_out, None, SUBLANES, tc), lambda t, j: (0, t, 0, j))
    out = pl.pallas_call(
        _fin_kernel,
        grid=(nt, d // tc),
        in_specs=[pl.BlockSpec((rows_in, None, SUBLANES, tc), lambda t, j: (0, t, 0, j)),
                  pl.BlockSpec((None, SUBLANES * rows_out, SUBLANES * rows_in), lambda t, j: (t, 0, 0)),
                  vspec, vspec, pl.BlockSpec((1, 1, tc), lambda t, j: (0, 0, j))],
        out_specs=vspec,
        out_shape=jax.ShapeDtypeStruct((rows_out, nt, SUBLANES, d), F32),
        compiler_params=_cp(("parallel", "arbitrary"), 48),
        name="fft_final",
    )(t4.reshape(rows_in, nt, SUBLANES, d), g_fin, vg.reshape(rows_out, nt, SUBLANES, d),
      x0.reshape(rows_out, nt, SUBLANES, d), skip.reshape(1, 1, d))
    return out.reshape(bsz, l, d)


def _hyena_longconv_main(z, hp, tables):
    (short_w, short_b, fparams, skip) = hp
    g_data, g_filt, g_fin, f2_fwd, f2_inv = tables
    bsz, l, d3 = z.shape
    d = d3 // 3
    n = 2 * l
    n1 = n // FFT_N2
    nt = FFT_N2 // SUBLANES
    vg, x0 = _hy_gate(z, short_w, short_b)
    kt = _hyena_filter(l, fparams, d)
    ky1 = _fft_stage1(kt.reshape(n1, nt, SUBLANES, d), g_filt)
    kspec = _fft_filter_spectrum(ky1.reshape(2 * n1, FFT_N2, d), f2_fwd, 1.0 / n)
    y1 = _fft_stage1(vg.reshape(bsz * l // FFT_N2, nt, SUBLANES, d), g_data)
    t4 = _fft_data_mid(y1.reshape(2 * n1, FFT_N2, d), kspec, f2_fwd, f2_inv)
    return _fft_final(t4, g_fin, vg, x0, skip)


def _ctxconv_kernel(vg_ref, x0_ref, k_ref, skip_ref, fd_ref, ff_ref, fi_ref, o_ref, *, scale):
    lc = vg_ref.shape[1]
    n = 2 * lc
    z = jnp.concatenate([vg_ref[0], vg_ref[1]], axis=0).astype(BF16)
    s = _bdot(fd_ref[...], z)
    ks = _bdot(ff_ref[...], k_ref[...].astype(BF16)) * scale
    sr, si = s[:n], s[n:]
    kr, ki = ks[:n], ks[n:]
    p = jnp.concatenate([sr * kr - si * ki, sr * ki + si * kr], axis=0).astype(BF16)
    y = _bdot(fi_ref[...], p)
    sk = skip_ref[...]
    o_ref[0] = (y[:lc] + vg_ref[0] * sk) * x0_ref[0]
    o_ref[1] = (y[lc:] + vg_ref[1] * sk) * x0_ref[1]


def _hyena_longconv_ctx(z, hp, tc_pref=512):
    (short_w, short_b, fparams, skip) = hp
    bsz, lc, d3 = z.shape
    d = d3 // 3
    n = 2 * lc
    vg, x0 = _hy_gate(z, short_w, short_b)
    kt = _hyena_filter(lc, fparams, d)
    kf = jnp.arange(n, dtype=jnp.int32)
    c, s = _cs(kf[:, None] * kf[None, :], n)
    cd, sd = c[:, :lc], s[:, :lc]
    fd = _blk(cd, sd, -sd, cd).astype(BF16)
    ff = jnp.concatenate([c, -s], axis=0).astype(BF16)
    fi = _blk(cd.T, -sd.T, sd.T, cd.T).astype(BF16)
    tc = _tile(d, tc_pref)
    xspec = pl.BlockSpec((bsz, lc, tc), lambda j: (0, 0, j))
    full = lambda a: pl.BlockSpec(a.shape, lambda j: (0, 0))
    return pl.pallas_call(
        functools.partial(_ctxconv_kernel, scale=1.0 / n),
        grid=(d // tc,),
        in_specs=[xspec, xspec, pl.BlockSpec((n, tc), lambda j: (0, j)), pl.BlockSpec((1, tc), lambda j: (0, j)),
                  full(fd), full(ff), full(fi)],
        out_specs=xspec,
        out_shape=jax.ShapeDtypeStruct((bsz, lc, d), F32),
        compiler_params=_cp(("parallel",), 40),
        name="hyena_ctx_conv",
    )(vg, x0, kt, skip.reshape(1, d), fd, ff, fi)


def _rope_tables(l, dh):
    rows = l // GRID_W
    d_axis = dh // 2
    inv_freq = ROPE_THETA ** (-jnp.arange(0, d_axis, 2, dtype=F32) / d_axis)
    ang_r = jnp.arange(rows, dtype=F32)[:, None] * inv_freq
    ang_c = jnp.arange(GRID_W, dtype=F32)[:, None] * inv_freq
    half = d_axis // 2
    ang = jnp.concatenate([jnp.broadcast_to(ang_r[:, None, :], (rows, GRID_W, half)),
                           jnp.broadcast_to(ang_c[None, :, :], (rows, GRID_W, half))], axis=-1)
    ang = ang.reshape(rows * GRID_W, d_axis)
    cos, sin = jnp.cos(ang), jnp.sin(ang)
    cosf = jnp.repeat(cos, 2, axis=-1)
    sinf = jnp.stack([-sin, sin], axis=-1).reshape(rows * GRID_W, dh)
    return cosf, sinf


def _prep_kernel(x_ref, cos_ref, sin_ref, qg_ref, kg_ref, *outs, nq, nkv, dh, qscale):
    cosf = cos_ref[...]
    sinf = sin_ref[...]
    even = (lax.broadcasted_iota(jnp.int32, cosf.shape, 1) % 2) == 0

    def norm_rope(xh, g):
        y = (xh * lax.rsqrt(jnp.mean(xh * xh, axis=-1, keepdims=True) + NORM_EPS)) * g
        sw = jnp.where(even, pltpu.roll(y, dh - 1, 1), pltpu.roll(y, 1, 1))
        return y * cosf + sw * sinf

    if nq:
        q_ref, k_ref, vt_ref = outs
        for h in range(nq):
            q_ref[:, h * dh:(h + 1) * dh] = (norm_rope(x_ref[:, h * dh:(h + 1) * dh], qg_ref[...]) * qscale).astype(BF16)
    else:
        k_ref, vt_ref = outs
    for j in range(nkv):
        c0 = (nq + j) * dh
        k_ref[j] = norm_rope(x_ref[:, c0:c0 + dh], kg_ref[...]).astype(BF16)
    for j in range(nkv):
        c0 = (nq + nkv + j) * dh
        vt_ref[j] = x_ref[:, c0:c0 + dh].T.astype(BF16)


def _attn_prep(x, cosf, sinf, q_g, k_g, nq, nkv, dh, tm_pref=256):
    bsz, l, cols = x.shape
    tm = _tile(l, tm_pref)
    out_specs = [pl.BlockSpec((None, nkv, tm, dh), lambda b, i: (b, 0, i, 0)),
                 pl.BlockSpec((None, nkv, dh, tm), lambda b, i: (b, 0, 0, i))]
    out_shape = [jax.ShapeDtypeStruct((bsz, nkv, l, dh), BF16), jax.ShapeDtypeStruct((bsz, nkv, dh, l), BF16)]
    if nq:
        out_specs = [pl.BlockSpec((None, tm, nq * dh), lambda b, i: (b, i, 0))] + out_specs
        out_shape = [jax.ShapeDtypeStruct((bsz, l, nq * dh), BF16)] + out_shape
    tab = pl.BlockSpec((tm, dh), lambda b, i: (i, 0))
    gain = pl.BlockSpec((1, dh), lambda b, i: (0, 0))
    return pl.pallas_call(
        functools.partial(_prep_kernel, nq=nq, nkv=nkv, dh=dh, qscale=dh ** -0.5),
        grid=(bsz, l // tm),
        in_specs=[pl.BlockSpec((None, tm, cols), lambda b, i: (b, i, 0)), tab, tab, gain, gain],
        out_specs=out_specs,
        out_shape=out_shape,
        compiler_params=_cp(("parallel", "parallel"), 40),
        name="attn_prep",
    )(x, cosf, sinf, q_g.reshape(1, dh), k_g.reshape(1, dh))


def _flash_kernel(q_ref, k_ref, vt_ref, o_ref, m_scr, l_scr, acc_scr, *, group, dh):
    kk = pl.program_id(3)

    @pl.when(kk == 0)
    def _():
        m_scr[...] = jnp.full_like(m_scr, -jnp.inf)
        l_scr[...] = jnp.zeros_like(l_scr)
        acc_scr[...] = jnp.zeros_like(acc_scr)

    k = k_ref[...]
    vt = vt_ref[...]
    for h in range(group):
        qh = q_ref[:, h * dh:(h + 1) * dh]
        s = lax.dot_general(k, qh, (((1,), (1,)), ((), ())), preferred_element_type=F32)
        m_old = m_scr[h]
        m_new = jnp.maximum(m_old, jnp.max(s, axis=0, keepdims=True))
        alpha = jnp.exp(m_old - m_new)
        p = jnp.exp(s - m_new)
        l_scr[h] = alpha * l_scr[h] + jnp.sum(p, axis=0, keepdims=True)
        acc_scr[h] = alpha * acc_scr[h] + _bdot(vt, p.astype(BF16))
        m_scr[h] = m_new

    @pl.when(kk == pl.num_programs(3) - 1)
    def _():
        for h in range(group):
            o = acc_scr[h] / l_scr[h]
            o_ref[:, h * dh:(h + 1) * dh] = o.T.astype(o_ref.dtype)


def _key_tile(lk, pref):
    best = LANES
    t = LANES
    while t <= min(lk, pref):
        if lk % t == 0:
            best = t
        t += LANES
    return best


def _flash_attention(q, k, vt, group, dh, tq_pref=512, tk_pref=1280):
    bsz, l, hd = q.shape
    nkv, lk = k.shape[1], k.shape[2]
    tq = _tile(l, tq_pref)
    tk = _key_tile(lk, tk_pref)
    gw = group * dh
    return pl.pallas_call(
        functools.partial(_flash_kernel, group=group, dh=dh),
        grid=(bsz, nkv, l // tq, lk // tk),
        in_specs=[
            pl.BlockSpec((None, tq, gw), lambda b, g, i, kk: (b, i, g)),
            pl.BlockSpec((None, None, tk, dh), lambda b, g, i, kk: (b, g, kk, 0)),
            pl.BlockSpec((None, None, dh, tk), lambda b, g, i, kk: (b, g, 0, kk)),
        ],
        out_specs=pl.BlockSpec((None, tq, gw), lambda b, g, i, kk: (b, i, g)),
        out_shape=jax.ShapeDtypeStruct((bsz, l, hd), BF16),
        scratch_shapes=[pltpu.VMEM((group, 1, tq), F32), pltpu.VMEM((group, 1, tq), F32),
                        pltpu.VMEM((group, dh, tq), F32)],
        compiler_params=_cp(("parallel", "parallel", "parallel", "arbitrary"), 48),
        name="flash_attention",
    )(q, k, vt)


def kernel(x, c, ctx, c_ctx, norm1_g, norm2_g, w_mod, b_mod, ffn_w_gate, ffn_w_up, ffn_w_down, hy_w_in, hy_b_in,
           hy_short_w, hy_short_b, hy_f_w1, hy_f_b1, hy_f_w2, hy_f_b2, hy_f_w3, hy_f_b3, hy_f_w4, hy_f_freq,
           hy_skip, hy_w_out, hy_b_out, sc_w_in, sc_conv_w, sc_w_out, at_w_qkv, at_q_g, at_k_g, at_w_o, final_g):
    bsz, l, d = x.shape
    lc = ctx.shape[1]
    depth = norm1_g.shape[0]
    dh = at_q_g.shape[-1]
    n_heads = d // dh
    n_kv = (at_w_qkv.shape[-1] - d) // (2 * dh)
    group = n_heads // n_kv
    assert bsz == 2, "the long convolution packs exactly two batch samples into one complex signal"
    assert l % FFT_N2 == 0 and l % GRID_W == 0

    attn_layers = [i for i in range(depth) if i % N_MIXERS == 2]
    last_ctx_reader = attn_layers[-1] if attn_layers else -1

    cond8 = jnp.zeros((SUBLANES, d), F32).at[:bsz].set(c).at[bsz].set(c_ctx)
    mods = _adaln_all(cond8, w_mod, b_mod)

    def mod_set(i, ctx_rows):
        m = mods[i]
        if ctx_rows:
            rows = jnp.broadcast_to(m[bsz:bsz + 1], (bsz, N_MOD * d))
        else:
            rows = m[:bsz]
        return [rows[:, None, k * d:(k + 1) * d] for k in range(N_MOD)]

    zeros = lambda n: jnp.zeros((n,), F32)
    bf = lambda w: w.astype(BF16)
    tables = None
    if any(i % N_MIXERS == 0 for i in range(depth)):
        tables = _fft_tables(l)

    xc = ctx
    for i in range(depth):
        kind, j = i % N_MIXERS, i // N_MIXERS
        ctx_in = i <= last_ctx_reader
        ctx_out = i < last_ctx_reader
        sh1, sc1, g1, sh2, sc2, g2 = mod_set(i, False)
        if ctx_in:
            csh1, csc1, cg1, csh2, csc2, cg2 = mod_set(i, True)
        wg, wu, wd = bf(ffn_w_gate[i]), bf(ffn_w_up[i]), bf(ffn_w_down[i])
        if kind == 0:
            w_in, w_out = bf(hy_w_in[j]), bf(hy_w_out[j])
            hp = (hy_short_w[j], hy_short_b[j],
                  (hy_f_w1[j], hy_f_b1[j], hy_f_w2[j], hy_f_b2[j], hy_f_w3[j], hy_f_b3[j], hy_f_w4[j], hy_f_freq[j]),
                  hy_skip[j])
            z = _norm_mod_matmul(x, norm1_g[i], sh1, sc1, w_in, hy_b_in[j])
            y = _hyena_longconv_main(z, hp, tables)
            x = _out_proj(y, w_out, hy_b_out[j], x, g1)
            if ctx_out:
                zc = _norm_mod_matmul(xc, norm1_g[i], csh1, csc1, w_in, hy_b_in[j])
                yc = _hyena_longconv_ctx(zc, hp)
                xc = _out_proj(yc, w_out, hy_b_out[j], xc, cg1)
        elif kind == 1:
            w_in, w_out = bf(sc_w_in[j]), bf(sc_w_out[j])
            z = _norm_mod_matmul(x, norm1_g[i], sh1, sc1, w_in, zeros(3 * d))
            x = _out_proj(_sc_gate(z, sc_conv_w[j]), w_out, zeros(d), x, g1)
            if ctx_out:
                zc = _norm_mod_matmul(xc, norm1_g[i], csh1, csc1, w_in, zeros(3 * d))
                xc = _out_proj(_sc_gate(zc, sc_conv_w[j]), w_out, zeros(d), xc, cg1)
        else:
            w_qkv, w_o = bf(at_w_qkv[j]), bf(at_w_o[j])
            qkv_cols = w_qkv.shape[1]
            qkv = _norm_mod_matmul(x, norm1_g[i], sh1, sc1, w_qkv, zeros(qkv_cols))
            cosf, sinf = _rope_tables(l, dh)
            q, k, vt = _attn_prep(qkv, cosf, sinf, at_q_g[j], at_k_g[j], n_heads, n_kv, dh)
            if ctx_out:
                qkv_c = _norm_mod_matmul(xc, norm1_g[i], csh1, csc1, w_qkv, zeros(qkv_cols))
                qc, kc, vtc = _attn_prep(qkv_c, jnp.ones((lc, dh), F32), jnp.zeros((lc, dh), F32),
                                         at_q_g[j], at_k_g[j], n_heads, n_kv, dh)
                oc = _flash_attention(qc, kc, vtc, group, dh)
                xc_next = _out_proj(oc, w_o, zeros(d), xc, cg1)
            else:
                kv_c = _norm_mod_matmul(xc, norm1_g[i], csh1, csc1, w_qkv[:, n_heads * dh:], zeros(qkv_cols - d))
                kc, vtc = _attn_prep(kv_c, jnp.ones((lc, dh), F32), jnp.zeros((lc, dh), F32),
                                     at_q_g[j], at_k_g[j], 0, n_kv, dh)
            k_all = jnp.concatenate([k, kc], axis=2)
            vt_all = jnp.concatenate([vt, vtc], axis=3)
            o = _flash_attention(q, k_all, vt_all, group, dh)
            x = _out_proj(o, w_o, zeros(d), x, g1)
            if ctx_out:
                xc = xc_next
        x = _ffn(x, norm2_g[i], sh2, sc2, g2, wg, wu, wd)
        if ctx_out:
            xc = _ffn(xc, norm2_g[i], csh2, csc2, cg2, wg, wu, wd)
    return _final_norm(x, final_g)
```

```python
import functools
import math

import jax
import jax.numpy as jnp
from jax import lax
from jax.experimental import pallas as pl
from jax.experimental.pallas import tpu as pltpu

F32 = jnp.float32
BF16 = jnp.bfloat16
NORM_EPS = 1e-6
GRID_W = 64
ROPE_THETA = 10000.0
HYENA_FAST_DECAY = 0.3
HYENA_SLOW_DECAY = 1.5
HYENA_TARGET = 1e-2
N_MOD = 6
N_MIXERS = 3
FFT_N2 = 256
LANES = 128
SUBLANES = 8
HIGHEST = lax.Precision.HIGHEST


def _cp(sem, vmem_mb=None):
    kw = dict(dimension_semantics=sem)
    if vmem_mb is not None:
        kw["vmem_limit_bytes"] = vmem_mb << 20
    return pltpu.CompilerParams(**kw)


def _tile(n, pref):
    if n <= pref:
        return n
    t = pref
    while n % t:
        t //= 2
    return t


def _bdot(a, b):
    return jnp.dot(a, b, preferred_element_type=F32)


def _adaln_kernel(c_ref, w_ref, b_ref, o_ref):
    c = c_ref[...]
    s = (c * jax.nn.sigmoid(c)).astype(BF16)
    o_ref[...] = _bdot(s, w_ref[...].astype(BF16)) + b_ref[...]


def _adaln_all(cond8, w_mod, b_mod):
    depth, d, n = w_mod.shape
    tn = _tile(n, 1024)
    return pl.pallas_call(
        _adaln_kernel,
        grid=(depth, n // tn),
        in_specs=[
            pl.BlockSpec((SUBLANES, d), lambda l, j: (0, 0)),
            pl.BlockSpec((None, d, tn), lambda l, j: (l, 0, j)),
            pl.BlockSpec((None, 1, tn), lambda l, j: (l, 0, j)),
        ],
        out_specs=pl.BlockSpec((None, SUBLANES, tn), lambda l, j: (l, 0, j)),
        out_shape=jax.ShapeDtypeStruct((depth, SUBLANES, n), F32),
        compiler_params=_cp(("parallel", "parallel"), 40),
        name="adaln",
    )(cond8, w_mod, b_mod.reshape(depth, 1, n))


def _norm_mod(x, g, sh, sc):
    rs = lax.rsqrt(jnp.mean(x * x, axis=-1, keepdims=True) + NORM_EPS)
    return ((x * rs) * g) * (1.0 + sc) + sh


def _nm_kernel(x_ref, g_ref, sh_ref, sc_ref, w_ref, b_ref, o_ref, a_scr):
    @pl.when(pl.program_id(2) == 0)
    def _():
        a_scr[...] = _norm_mod(x_ref[...], g_ref[...], sh_ref[...], sc_ref[...]).astype(BF16)

    o_ref[...] = (_bdot(a_scr[...], w_ref[...]) + b_ref[...]).astype(o_ref.dtype)


def _norm_mod_matmul(x, g, sh, sc, w, bias, tm_pref=512, tn_pref=2048):
    bsz, l, d = x.shape
    n = w.shape[1]
    tm, tn = _tile(l, tm_pref), _tile(n, tn_pref)
    return pl.pallas_call(
        _nm_kernel,
        grid=(bsz, l // tm, n // tn),
        in_specs=[
            pl.BlockSpec((None, tm, d), lambda b, i, j: (b, i, 0)),
            pl.BlockSpec((1, d), lambda b, i, j: (0, 0)),
            pl.BlockSpec((None, 1, d), lambda b, i, j: (b, 0, 0)),
            pl.BlockSpec((None, 1, d), lambda b, i, j: (b, 0, 0)),
            pl.BlockSpec((d, tn), lambda b, i, j: (0, j)),
            pl.BlockSpec((1, tn), lambda b, i, j: (0, j)),
        ],
        out_specs=pl.BlockSpec((None, tm, tn), lambda b, i, j: (b, i, j)),
        out_shape=jax.ShapeDtypeStruct((bsz, l, n), F32),
        scratch_shapes=[pltpu.VMEM((tm, d), BF16)],
        compiler_params=_cp(("parallel", "parallel", "arbitrary"), 52),
        name="norm_mod_matmul",
    )(x, g.reshape(1, d), sh, sc, w, bias.reshape(1, n))


def _ffn_kernel(x_ref, g_ref, sh_ref, sc_ref, gate_ref, wg_ref, wu_ref, wd_ref, o_ref, a_scr, acc_scr):
    f = pl.program_id(2)

    @pl.when(f == 0)
    def _():
        a_scr[...] = _norm_mod(x_ref[...], g_ref[...], sh_ref[...], sc_ref[...]).astype(BF16)
        acc_scr[...] = jnp.zeros_like(acc_scr)

    a = a_scr[...]
    hg = _bdot(a, wg_ref[...])
    hu = _bdot(a, wu_ref[...])
    h = (hg * jax.nn.sigmoid(hg)) * hu
    acc_scr[...] += _bdot(h.astype(BF16), wd_ref[...])

    @pl.when(f == pl.num_programs(2) - 1)
    def _():
        o_ref[...] = x_ref[...] + gate_ref[...] * acc_scr[...]


def _ffn(x, g, sh, sc, gate, wg, wu, wd, tm_pref=512, tf_pref=512):
    bsz, l, d = x.shape
    fdim = wg.shape[1]
    tm, tf = _tile(l, tm_pref), _tile(fdim, tf_pref)
    return pl.pallas_call(
        _ffn_kernel,
        grid=(bsz, l // tm, fdim // tf),
        in_specs=[
            pl.BlockSpec((None, tm, d), lambda b, i, f: (b, i, 0)),
            pl.BlockSpec((1, d), lambda b, i, f: (0, 0)),
            pl.BlockSpec((None, 1, d), lambda b, i, f: (b, 0, 0)),
            pl.BlockSpec((None, 1, d), lambda b, i, f: (b, 0, 0)),
            pl.BlockSpec((None, 1, d), lambda b, i, f: (b, 0, 0)),
            pl.BlockSpec((d, tf), lambda b, i, f: (0, f)),
            pl.BlockSpec((d, tf), lambda b, i, f: (0, f)),
            pl.BlockSpec((tf, d), lambda b, i, f: (f, 0)),
        ],
        out_specs=pl.BlockSpec((None, tm, d), lambda b, i, f: (b, i, 0)),
        out_shape=jax.ShapeDtypeStruct((bsz, l, d), F32),
        scratch_shapes=[pltpu.VMEM((tm, d), BF16), pltpu.VMEM((tm, d), F32)],
        compiler_params=_cp(("parallel", "parallel", "arbitrary"), 48),
        name="ffn",
    )(x, g.reshape(1, d), sh, sc, gate, wg, wu, wd)


def _out_kernel(y_ref, w_ref, b_ref, x_ref, gate_ref, o_ref):
    o_ref[...] = x_ref[...] + gate_ref[...] * (_bdot(y_ref[...].astype(BF16), w_ref[...]) + b_ref[...])


def _out_proj(y, w, bias, x, gate, tm_pref=512, tn_pref=1024):
    bsz, l, d = x.shape
    k = w.shape[0]
    tm, tn = _tile(l, tm_pref), _tile(d, tn_pref)
    return pl.pallas_call(
        _out_kernel,
        grid=(bsz, l // tm, d // tn),
        in_specs=[
            pl.BlockSpec((None, tm, k), lambda b, i, j: (b, i, 0)),
            pl.BlockSpec((k, tn), lambda b, i, j: (0, j)),
            pl.BlockSpec((1, tn), lambda b, i, j: (0, j)),
            pl.BlockSpec((None, tm, tn), lambda b, i, j: (b, i, j)),
            pl.BlockSpec((None, 1, tn), lambda b, i, j: (b, 0, j)),
        ],
        out_specs=pl.BlockSpec((None, tm, tn), lambda b, i, j: (b, i, j)),
        out_shape=jax.ShapeDtypeStruct((bsz, l, d), F32),
        compiler_params=_cp(("parallel", "parallel", "parallel"), 40),
        name="out_proj",
    )(y, w, bias.reshape(1, d), x, gate)


def _final_norm_kernel(x_ref, g_ref, o_ref):
    x = x_ref[...]
    o_ref[...] = (x * lax.rsqrt(jnp.mean(x * x, axis=-1, keepdims=True) + NORM_EPS)) * g_ref[...]


def _final_norm(x, g, tm_pref=512):
    bsz, l, d = x.shape
    tm = _tile(l, tm_pref)
    return pl.pallas_call(
        _final_norm_kernel,
        grid=(bsz, l // tm),
        in_specs=[pl.BlockSpec((None, tm, d), lambda b, i: (b, i, 0)), pl.BlockSpec((1, d), lambda b, i: (0, 0))],
        out_specs=pl.BlockSpec((None, tm, d), lambda b, i: (b, i, 0)),
        out_shape=jax.ShapeDtypeStruct((bsz, l, d), F32),
        compiler_params=_cp(("parallel", "parallel")),
        name="final_norm",
    )(x, g.reshape(1, d))


def _conv3(x, prev_row, next_row, w):
    tm = x.shape[0]
    row = lax.broadcasted_iota(jnp.int32, x.shape, 0)
    up = jnp.where(row == 0, prev_row, pltpu.roll(x, 1, 0))
    dn = jnp.where(row == tm - 1, next_row, pltpu.roll(x, tm - 1, 0))
    return up * w[0:1] + x * w[1:2] + dn * w[2:3]


def _halo_rows(p_ref, n_ref):
    i = pl.program_id(1)
    last = pl.num_programs(1) - 1
    prev_row = jnp.where(i == 0, 0.0, p_ref[SUBLANES - 1:SUBLANES, :])
    next_row = jnp.where(i == last, 0.0, n_ref[0:1, :])
    return prev_row, next_row


def _hy_gate_kernel(m0, p0, n0, m1, p1, n1, m2, p2, n2, w0, w1, w2, b0, b1, b2, vg_ref, x0_ref):
    def comp(m, p, n, w, b):
        pr, nx = _halo_rows(p, n)
        return _conv3(m[...], pr, nx, w[...]) + b[...]

    u0 = comp(m0, p0, n0, w0, b0)
    u1 = comp(m1, p1, n1, w1, b1)
    u2 = comp(m2, p2, n2, w2, b2)
    vg_ref[...] = u2 * u1
    x0_ref[...] = u0


def _sc_gate_kernel(m0, m1, p1, n1, m2, p2, n2, w_ref, o_ref):
    i = pl.program_id(1)
    last = pl.num_programs(1) - 1
    prev_row = jnp.where(i == 0, 0.0, p1[SUBLANES - 1:SUBLANES, :] * p2[SUBLANES - 1:SUBLANES, :])
    next_row = jnp.where(i == last, 0.0, n1[0:1, :] * n2[0:1, :])
    o_ref[...] = (m0[...] * _conv3(m1[...] * m2[...], prev_row, next_row, w_ref[...])).astype(o_ref.dtype)


def _halo_specs(tm, tc, l, col):
    r = tm // SUBLANES
    nblk = l // SUBLANES
    return [
        pl.BlockSpec((None, tm, tc), lambda b, i, j: (b, i, col + j)),
        pl.BlockSpec((None, SUBLANES, tc), lambda b, i, j: (b, jnp.maximum(i * r - 1, 0), col + j)),
        pl.BlockSpec((None, SUBLANES, tc), lambda b, i, j: (b, jnp.minimum((i + 1) * r, nblk - 1), col + j)),
    ]


def _hy_gate(z, short_w, short_b, tc_pref=1024):
    bsz, l, d3 = z.shape
    d = d3 // 3
    tm = _tile(l, 256)
    tc = _tile(d, tc_pref)
    nd = d // tc
    in_specs = []
    for comp in range(3):
        in_specs += _halo_specs(tm, tc, l, comp * nd)
    for comp in range(3):
        in_specs.append(pl.BlockSpec((3, tc), lambda b, i, j, c=comp: (0, c * nd + j)))
    for comp in range(3):
        in_specs.append(pl.BlockSpec((1, tc), lambda b, i, j, c=comp: (0, c * nd + j)))
    ospec = pl.BlockSpec((None, tm, tc), lambda b, i, j: (b, i, j))
    return pl.pallas_call(
        _hy_gate_kernel,
        grid=(bsz, l // tm, nd),
        in_specs=in_specs,
        out_specs=[ospec, ospec],
        out_shape=[jax.ShapeDtypeStruct((bsz, l, d), F32)] * 2,
        compiler_params=_cp(("parallel", "parallel", "parallel"), 40),
        name="hyena_gate",
    )(*([z] * 9), short_w, short_w, short_w, *([short_b.reshape(1, d3)] * 3))


def _sc_gate(z, conv_w, tc_pref=1024):
    bsz, l, d3 = z.shape
    d = d3 // 3
    tm = _tile(l, 256)
    tc = _tile(d, tc_pref)
    nd = d // tc
    in_specs = [_halo_specs(tm, tc, l, 0)[0]] + _halo_specs(tm, tc, l, nd) + _halo_specs(tm, tc, l, 2 * nd)
    in_specs.append(pl.BlockSpec((3, tc), lambda b, i, j: (0, j)))
    return pl.pallas_call(
        _sc_gate_kernel,
        grid=(bsz, l // tm, nd),
        in_specs=in_specs,
        out_specs=pl.BlockSpec((None, tm, tc), lambda b, i, j: (b, i, j)),
        out_shape=jax.ShapeDtypeStruct((bsz, l, d), BF16),
        compiler_params=_cp(("parallel", "parallel", "parallel"), 40),
        name="shortconv_gate",
    )(*([z] * 7), conv_w)


def _filter_features(l):
    emb = 33
    bands = (emb - 1) // 2
    t = jnp.linspace(0.0, 1.0, l, dtype=F32)[:, None]
    omega = 2.0 * math.pi * jnp.arange(l, dtype=F32)[:, None] / l
    fb = jnp.linspace(1e-4, bands - 1, bands, dtype=F32)[None, :]
    z = jnp.concatenate([t, jnp.cos(fb * omega), -jnp.sin(fb * omega)], axis=-1)
    z = jnp.pad(z, ((0, 0), (0, LANES - emb)))
    rows = jnp.arange(2 * l, dtype=jnp.int32)
    pos = jnp.where(rows < l, rows, 2 * l - rows)
    return z[jnp.clip(pos, 0, l - 1)]


def _filt_kernel(z_ref, w1, b1, w2, b2, w3, b3, fr, w4h, w4l, dab, o_ref):
    z = z_ref[...]

    def hdot(a, b):
        return jnp.dot(a, b, precision=HIGHEST, preferred_element_type=F32)

    f = fr[...]
    h = jnp.sin(f * (hdot(z, w1[...]) + b1[...]))
    h = jnp.sin(f * (hdot(h, w2[...]) + b2[...]))
    h = jnp.sin(f * (hdot(h, w3[...]) + b3[...]))
    h_hi = h.astype(BF16)
    h_lo = (h - h_hi.astype(F32)).astype(BF16)
    hw = _bdot(h_hi, w4h[...]) + _bdot(h_lo, w4h[...]) + _bdot(h_hi, w4l[...])
    k = hw * jnp.exp(-(z[:, 0:1] * dab[...]))
    row = lax.broadcasted_iota(jnp.int32, k.shape, 0)
    first_bwd = pl.program_id(0) * 2 == pl.num_programs(0)
    o_ref[...] = jnp.where((row == 0) & first_bwd, 0.0, k)


def _hyena_filter(l, p, d, tc_pref=2048):
    w1, b1, w2, b2, w3, b3, w4, freq = p
    order = w2.shape[0]
    zf = _filter_features(l)

    def pad2(a, rows, cols):
        return jnp.pad(a, ((0, rows - a.shape[0]), (0, cols - a.shape[1])))

    w1p = pad2(w1, LANES, LANES)
    w2p = pad2(w2, LANES, LANES)
    w3p = pad2(w3, LANES, LANES)
    b1p = pad2(b1.reshape(1, order), 1, LANES)
    b2p = pad2(b2.reshape(1, order), 1, LANES)
    b3p = pad2(b3.reshape(1, order), 1, LANES)
    frp = pad2(freq.reshape(1, order), 1, LANES)
    w4p = pad2(w4, LANES, 2 * d)
    w4h = w4p.astype(BF16)
    w4l = (w4p - w4h.astype(F32)).astype(BF16)
    max_decay = math.log(HYENA_TARGET) / HYENA_FAST_DECAY
    min_decay = math.log(HYENA_TARGET) / HYENA_SLOW_DECAY
    dab = jnp.abs(jnp.linspace(min_decay, max_decay, d, dtype=F32)).reshape(1, d)
    tc = _tile(d, tc_pref)
    nd = d // tc
    r = _tile(l, 512)
    nb = 2 * l // r
    small = pl.BlockSpec((LANES, LANES), lambda i, j: (0, 0))
    vec = pl.BlockSpec((1, LANES), lambda i, j: (0, 0))
    bank = pl.BlockSpec((LANES, tc), lambda i, j: (0, (i * 2 // nb) * nd + j))
    return pl.pallas_call(
        _filt_kernel,
        grid=(nb, nd),
        in_specs=[pl.BlockSpec((r, LANES), lambda i, j: (i, 0)),
                  small, vec, small, vec, small, vec, vec, bank, bank, pl.BlockSpec((1, tc), lambda i, j: (0, j))],
        out_specs=pl.BlockSpec((r, tc), lambda i, j: (i, j)),
        out_shape=jax.ShapeDtypeStruct((2 * l, d), F32),
        compiler_params=_cp(("parallel", "parallel"), 40),
        name="hyena_filter",
    )(zf, w1p, b1p, w2p, b2p, w3p, b3p, frp, w4h, w4l, dab)


def _cs(m, n):
    ang = (m % n).astype(F32) * (2.0 * math.pi / n)
    return jnp.cos(ang), jnp.sin(ang)


def _blk(a, b, c, d):
    return jnp.concatenate([jnp.concatenate([a, b], axis=-1), jnp.concatenate([c, d], axis=-1)], axis=-2)


def _fft_tables(l):
    n = 2 * l
    n1 = n // FFT_N2
    a_cnt = n1 // 2
    k1 = jnp.arange(n1, dtype=jnp.int32)
    b = jnp.arange(FFT_N2, dtype=jnp.int32)
    a_full = jnp.arange(n1, dtype=jnp.int32)
    pos = FFT_N2 * a_full[None, :] + b[:, None]
    c, s = _cs(k1[None, :, None] * pos[:, None, :], n)
    cd, sd = c[:, :, :a_cnt], s[:, :, :a_cnt]
    g_data = _blk(cd, sd, -sd, cd).astype(BF16)
    g_filt = jnp.concatenate([c, -s], axis=-2).astype(BF16)
    ct, st = jnp.swapaxes(cd, 1, 2), jnp.swapaxes(sd, 1, 2)
    g_fin = _blk(ct, -st, st, ct).astype(BF16)
    c2, s2 = _cs(b[:, None] * b[None, :], FFT_N2)
    f2_fwd = _blk(c2, s2, -s2, c2).astype(BF16)
    f2_inv = _blk(c2, -s2, s2, c2).astype(BF16)
    return g_data, g_filt, g_fin, f2_fwd, f2_inv


def _per_b_matmul(x, g_ref):
    xt = pltpu.einshape("rjc->jrc", x)
    ys = [_bdot(g_ref[j], xt[j].astype(BF16)) for j in range(SUBLANES)]
    return pltpu.einshape("jmc->mjc", jnp.stack(ys, axis=0))


def _s1_kernel(x_ref, g_ref, o_ref):
    o_ref[...] = _per_b_matmul(x_ref[...], g_ref)


def _fft_stage1(x4, g, tc_pref=512):
    r, nt, sub, d = x4.shape
    m = g.shape[1]
    tc = _tile(d, tc_pref)
    return pl.pallas_call(
        _s1_kernel,
        grid=(nt, d // tc),
        in_specs=[pl.BlockSpec((r, None, sub, tc), lambda t, j: (0, t, 0, j)),
                  pl.BlockSpec((sub, m, r), lambda t, j: (t, 0, 0))],
        out_specs=pl.BlockSpec((m, None, sub, tc), lambda t, j: (0, t, 0, j)),
        out_shape=jax.ShapeDtypeStruct((m, nt, sub, d), F32),
        compiler_params=_cp(("parallel", "arbitrary"), 48),
        name="fft_stage1",
    )(x4, g)


def _fmid_kernel(y_ref, f_ref, o_ref, *, scale):
    y = jnp.concatenate([y_ref[0], y_ref[1]], axis=0).astype(BF16)
    o_ref[...] = _bdot(f_ref[...], y) * scale


def _fft_filter_spectrum(y1, f2_fwd, scale, tc_pref=1024):
    m, n2, d = y1.shape
    n1 = m // 2
    tc = _tile(d, tc_pref)
    y4 = y1.reshape(2, n1, n2, d)
    return pl.pallas_call(
        functools.partial(_fmid_kernel, scale=scale),
        grid=(n1, d // tc),
        in_specs=[pl.BlockSpec((2, None, n2, tc), lambda k, j: (0, k, 0, j)),
                  pl.BlockSpec((2 * n2, 2 * n2), lambda k, j: (0, 0))],
        out_specs=pl.BlockSpec((None, 2 * n2, tc), lambda k, j: (k, 0, j)),
        out_shape=jax.ShapeDtypeStruct((n1, 2 * n2, d), F32),
        compiler_params=_cp(("parallel", "parallel"), 40),
        name="fft_filter_stage2",
    )(y4, f2_fwd)


def _dmid_kernel(y_ref, k_ref, ff_ref, fi_ref, t_ref):
    n2 = y_ref.shape[1]
    y = jnp.concatenate([y_ref[0], y_ref[1]], axis=0).astype(BF16)
    s = _bdot(ff_ref[...], y)
    sr, si = s[:n2], s[n2:]
    kr, ki = k_ref[:n2, :], k_ref[n2:, :]
    p = jnp.concatenate([sr * kr - si * ki, sr * ki + si * kr], axis=0).astype(BF16)
    t = _bdot(fi_ref[...], p)
    t_ref[0] = t[:n2]
    t_ref[1] = t[n2:]


def _fft_data_mid(y1, kspec, f2_fwd, f2_inv, tc_pref=1024):
    m, n2, d = y1.shape
    n1 = m // 2
    tc = _tile(d, tc_pref)
    y4 = y1.reshape(2, n1, n2, d)
    tbl = pl.BlockSpec((2 * n2, 2 * n2), lambda k, j: (0, 0))
    yspec = pl.BlockSpec((2, None, n2, tc), lambda k, j: (0, k, 0, j))
    return pl.pallas_call(
        _dmid_kernel,
        grid=(n1, d // tc),
        in_specs=[yspec, pl.BlockSpec((None, 2 * n2, tc), lambda k, j: (k, 0, j)), tbl, tbl],
        out_specs=yspec,
        out_shape=jax.ShapeDtypeStruct((2, n1, n2, d), F32),
        compiler_params=_cp(("parallel", "parallel"), 48),
        name="fft_data_mid",
    )(y4, kspec, f2_fwd, f2_inv)


def _fin_kernel(t_ref, g_ref, vg_ref, x0_ref, skip_ref, o_ref):
    conv = _per_b_matmul(t_ref[...], g_ref)
    o_ref[...] = (conv + vg_ref[...] * skip_ref[...]) * x0_ref[...]


def _fft_final(t4, g_fin, vg, x0, skip, tc_pref=512):
    _, n1, n2, d = t4.shape
    bsz, l, _ = vg.shape
    nt = n2 // SUBLANES
    rows_in = 2 * n1
    rows_out = bsz * l // n2
    tc = _tile(d, tc_pref)
    vspec = pl.BlockSpec((rows_out, None, SUBLANES, tc), lambda t, j: (0, t, 0, j))
    out = pl.pallas_call(
        _fin_kernel,
        grid=(nt, d // tc),
        in_specs=[pl.BlockSpec((rows_in, None, SUBLANES, tc), lambda t, j: (0, t, 0, j)),
                  pl.BlockSpec((SUBLANES, rows_out, rows_in), lambda t, j: (t, 0, 0)),
                  vspec, vspec, pl.BlockSpec((1, 1, tc), lambda t, j: (0, 0, j))],
        out_specs=vspec,
        out_shape=jax.ShapeDtypeStruct((rows_out, nt, SUBLANES, d), F32),
        compiler_params=_cp(("parallel", "arbitrary"), 48),
        name="fft_final",
    )(t4.reshape(rows_in, nt, SUBLANES, d), g_fin, vg.reshape(rows_out, nt, SUBLANES, d),
      x0.reshape(rows_out, nt, SUBLANES, d), skip.reshape(1, 1, d))
    return out.reshape(bsz, l, d)


def _hyena_longconv_main(z, hp, tables):
    (short_w, short_b, fparams, skip) = hp
    g_data, g_filt, g_fin, f2_fwd, f2_inv = tables
    bsz, l, d3 = z.shape
    d = d3 // 3
    n = 2 * l
    n1 = n // FFT_N2
    nt = FFT_N2 // SUBLANES
    vg, x0 = _hy_gate(z, short_w, short_b)
    kt = _hyena_filter(l, fparams, d)
    ky1 = _fft_stage1(kt.reshape(n1, nt, SUBLANES, d), g_filt)
    kspec = _fft_filter_spectrum(ky1.reshape(2 * n1, FFT_N2, d), f2_fwd, 1.0 / n)
    y1 = _fft_stage1(vg.reshape(bsz * l // FFT_N2, nt, SUBLANES, d), g_data)
    t4 = _fft_data_mid(y1.reshape(2 * n1, FFT_N2, d), kspec, f2_fwd, f2_inv)
    return _fft_final(t4, g_fin, vg, x0, skip)


def _ctxconv_kernel(vg_ref, x0_ref, k_ref, skip_ref, fd_ref, ff_ref, fi_ref, o_ref, *, scale):
    lc = vg_ref.shape[1]
    n = 2 * lc
    z = jnp.concatenate([vg_ref[0], vg_ref[1]], axis=0).astype(BF16)
    s = _bdot(fd_ref[...], z)
    ks = _bdot(ff_ref[...], k_ref[...].astype(BF16)) * scale
    sr, si = s[:n], s[n:]
    kr, ki = ks[:n], ks[n:]
    p = jnp.concatenate([sr * kr - si * ki, sr * ki + si * kr], axis=0).astype(BF16)
    y = _bdot(fi_ref[...], p)
    sk = skip_ref[...]
    o_ref[0] = (y[:lc] + vg_ref[0] * sk) * x0_ref[0]
    o_ref[1] = (y[lc:] + vg_ref[1] * sk) * x0_ref[1]


def _hyena_longconv_ctx(z, hp, tc_pref=512):
    (short_w, short_b, fparams, skip) = hp
    bsz, lc, d3 = z.shape
    d = d3 // 3
    n = 2 * lc
    vg, x0 = _hy_gate(z, short_w, short_b)
    kt = _hyena_filter(lc, fparams, d)
    kf = jnp.arange(n, dtype=jnp.int32)
    c, s = _cs(kf[:, None] * kf[None, :], n)
    cd, sd = c[:, :lc], s[:, :lc]
    fd = _blk(cd, sd, -sd, cd).astype(BF16)
    ff = jnp.concatenate([c, -s], axis=0).astype(BF16)
    fi = _blk(cd.T, -sd.T, sd.T, cd.T).astype(BF16)
    tc = _tile(d, tc_pref)
    xspec = pl.BlockSpec((bsz, lc, tc), lambda j: (0, 0, j))
    full = lambda a: pl.BlockSpec(a.shape, lambda j: (0, 0))
    return pl.pallas_call(
        functools.partial(_ctxconv_kernel, scale=1.0 / n),
        grid=(d // tc,),
        in_specs=[xspec, xspec, pl.BlockSpec((n, tc), lambda j: (0, j)), pl.BlockSpec((1, tc), lambda j: (0, j)),
                  full(fd), full(ff), full(fi)],
        out_specs=xspec,
        out_shape=jax.ShapeDtypeStruct((bsz, lc, d), F32),
        compiler_params=_cp(("parallel",), 40),
        name="hyena_ctx_conv",
    )(vg, x0, kt, skip.reshape(1, d), fd, ff, fi)


def _rope_tables(l, dh):
    rows = l // GRID_W
    d_axis = dh // 2
    inv_freq = ROPE_THETA ** (-jnp.arange(0, d_axis, 2, dtype=F32) / d_axis)
    ang_r = jnp.arange(rows, dtype=F32)[:, None] * inv_freq
    ang_c = jnp.arange(GRID_W, dtype=F32)[:, None] * inv_freq
    half = d_axis // 2
    ang = jnp.concatenate([jnp.broadcast_to(ang_r[:, None, :], (rows, GRID_W, half)),
                           jnp.broadcast_to(ang_c[None, :, :], (rows, GRID_W, half))], axis=-1)
    ang = ang.reshape(rows * GRID_W, d_axis)
    cos, sin = jnp.cos(ang), jnp.sin(ang)
    cosf = jnp.repeat(cos, 2, axis=-1)
    sinf = jnp.stack([-sin, sin], axis=-1).reshape(rows * GRID_W, dh)
    return cosf, sinf


def _prep_kernel(x_ref, cos_ref, sin_ref, qg_ref, kg_ref, *outs, nq, nkv, dh, qscale):
    cosf = cos_ref[...]
    sinf = sin_ref[...]
    even = (lax.broadcasted_iota(jnp.int32, cosf.shape, 1) % 2) == 0

    def norm_rope(xh, g):
        y = (xh * lax.rsqrt(jnp.mean(xh * xh, axis=-1, keepdims=True) + NORM_EPS)) * g
        sw = jnp.where(even, pltpu.roll(y, dh - 1, 1), pltpu.roll(y, 1, 1))
        return y * cosf + sw * sinf

    if nq:
        q_ref, k_ref, vt_ref = outs
        for h in range(nq):
            q_ref[:, h * dh:(h + 1) * dh] = (norm_rope(x_ref[:, h * dh:(h + 1) * dh], qg_ref[...]) * qscale).astype(BF16)
    else:
        k_ref, vt_ref = outs
    for j in range(nkv):
        c0 = (nq + j) * dh
        k_ref[j] = norm_rope(x_ref[:, c0:c0 + dh], kg_ref[...]).astype(BF16)
    for j in range(nkv):
        c0 = (nq + nkv + j) * dh
        vt_ref[j] = x_ref[:, c0:c0 + dh].T.astype(BF16)


def _attn_prep(x, cosf, sinf, q_g, k_g, nq, nkv, dh, tm_pref=256):
    bsz, l, cols = x.shape
    tm = _tile(l, tm_pref)
    out_specs = [pl.BlockSpec((None, nkv, tm, dh), lambda b, i: (b, 0, i, 0)),
                 pl.BlockSpec((None, nkv, dh, tm), lambda b, i: (b, 0, 0, i))]
    out_shape = [jax.ShapeDtypeStruct((bsz, nkv, l, dh), BF16), jax.ShapeDtypeStruct((bsz, nkv, dh, l), BF16)]
    if nq:
        out_specs = [pl.BlockSpec((None, tm, nq * dh), lambda b, i: (b, i, 0))] + out_specs
        out_shape = [jax.ShapeDtypeStruct((bsz, l, nq * dh), BF16)] + out_shape
    tab = pl.BlockSpec((tm, dh), lambda b, i: (i, 0))
    gain = pl.BlockSpec((1, dh), lambda b, i: (0, 0))
    return pl.pallas_call(
        functools.partial(_prep_kernel, nq=nq, nkv=nkv, dh=dh, qscale=dh ** -0.5 * math.log2(math.e)),
        grid=(bsz, l // tm),
        in_specs=[pl.BlockSpec((None, tm, cols), lambda b, i: (b, i, 0)), tab, tab, gain, gain],
        out_specs=out_specs,
        out_shape=out_shape,
        compiler_params=_cp(("parallel", "parallel"), 40),
        name="attn_prep",
    )(x, cosf, sinf, q_g.reshape(1, dh), k_g.reshape(1, dh))


def _flash_kernel(q_ref, k_ref, vt_ref, o_ref, m_scr, l_scr, acc_scr, *, group, dh):
    kk = pl.program_id(3)

    @pl.when(kk == 0)
    def _():
        m_scr[...] = jnp.full_like(m_scr, -jnp.inf)
        l_scr[...] = jnp.zeros_like(l_scr)
        acc_scr[...] = jnp.zeros_like(acc_scr)

    k = k_ref[...]
    vt = vt_ref[...]
    scores = [lax.dot_general(k, q_ref[:, h * dh:(h + 1) * dh], (((1,), (1,)), ((), ())),
                              preferred_element_type=F32) for h in range(group)]
    probs = []
    for h in range(group):
        s = scores[h]
        m_old = m_scr[h]
        m_new = jnp.maximum(m_old, jnp.max(s, axis=0, keepdims=True))
        alpha = jnp.exp2(m_old - m_new)
        p = jnp.exp2(s - m_new)
        l_scr[h] = alpha * l_scr[h] + jnp.sum(p, axis=0, keepdims=True)
        m_scr[h] = m_new
        probs.append((alpha, p.astype(BF16)))
    for h in range(group):
        alpha, p = probs[h]
        acc_scr[h] = alpha * acc_scr[h] + _bdot(vt, p)

    @pl.when(kk == pl.num_programs(3) - 1)
    def _():
        for h in range(group):
            o = acc_scr[h] / l_scr[h]
            o_ref[:, h * dh:(h + 1) * dh] = o.T.astype(o_ref.dtype)


def _key_tile(lk, pref):
    best = LANES
    t = LANES
    while t <= min(lk, pref):
        if lk % t == 0:
            best = t
        t += LANES
    return best


def _flash_attention(q, k, vt, group, dh, tq_pref=512, tk_pref=1280):
    bsz, l, hd = q.shape
    nkv, lk = k.shape[1], k.shape[2]
    tq = _tile(l, tq_pref)
    tk = _key_tile(lk, tk_pref)
    gw = group * dh
    return pl.pallas_call(
        functools.partial(_flash_kernel, group=group, dh=dh),
        grid=(bsz, nkv, l // tq, lk // tk),
        in_specs=[
            pl.BlockSpec((None, tq, gw), lambda b, g, i, kk: (b, i, g)),
            pl.BlockSpec((None, None, tk, dh), lambda b, g, i, kk: (b, g, kk, 0)),
            pl.BlockSpec((None, None, dh, tk), lambda b, g, i, kk: (b, g, 0, kk)),
        ],
        out_specs=pl.BlockSpec((None, tq, gw), lambda b, g, i, kk: (b, i, g)),
        out_shape=jax.ShapeDtypeStruct((bsz, l, hd), BF16),
        scratch_shapes=[pltpu.VMEM((group, 1, tq), F32), pltpu.VMEM((group, 1, tq), F32),
                        pltpu.VMEM((group, dh, tq), F32)],
        compiler_params=_cp(("parallel", "parallel", "parallel", "arbitrary"), 48),
        name="flash_attention",
    )(q, k, vt)


def kernel(x, c, ctx, c_ctx, norm1_g, norm2_g, w_mod, b_mod, ffn_w_gate, ffn_w_up, ffn_w_down, hy_w_in, hy_b_in,
           hy_short_w, hy_short_b, hy_f_w1, hy_f_b1, hy_f_w2, hy_f_b2, hy_f_w3, hy_f_b3, hy_f_w4, hy_f_freq,
           hy_skip, hy_w_out, hy_b_out, sc_w_in, sc_conv_w, sc_w_out, at_w_qkv, at_q_g, at_k_g, at_w_o, final_g):
    bsz, l, d = x.shape
    lc = ctx.shape[1]
    depth = norm1_g.shape[0]
    dh = at_q_g.shape[-1]
    n_heads = d // dh
    n_kv = (at_w_qkv.shape[-1] - d) // (2 * dh)
    group = n_heads // n_kv
    assert bsz == 2, "the long convolution packs exactly two batch samples into one complex signal"
    assert l % FFT_N2 == 0 and l % GRID_W == 0

    attn_layers = [i for i in range(depth) if i % N_MIXERS == 2]
    last_ctx_reader = attn_layers[-1] if attn_layers else -1

    cond8 = jnp.zeros((SUBLANES, d), F32).at[:bsz].set(c).at[bsz].set(c_ctx)
    mods = _adaln_all(cond8, w_mod, b_mod)

    def mod_set(i, ctx_rows):
        m = mods[i]
        if ctx_rows:
            rows = jnp.broadcast_to(m[bsz:bsz + 1], (bsz, N_MOD * d))
        else:
            rows = m[:bsz]
        return [rows[:, None, k * d:(k + 1) * d] for k in range(N_MOD)]

    zeros = lambda n: jnp.zeros((n,), F32)
    bf = lambda w: w.astype(BF16)
    tables = None
    if any(i % N_MIXERS == 0 for i in range(depth)):
        tables = _fft_tables(l)

    xc = ctx
    for i in range(depth):
        kind, j = i % N_MIXERS, i // N_MIXERS
        ctx_in = i <= last_ctx_reader
        ctx_out = i < last_ctx_reader
        sh1, sc1, g1, sh2, sc2, g2 = mod_set(i, False)
        if ctx_in:
            csh1, csc1, cg1, csh2, csc2, cg2 = mod_set(i, True)
        wg, wu, wd = bf(ffn_w_gate[i]), bf(ffn_w_up[i]), bf(ffn_w_down[i])
        if kind == 0:
            w_in, w_out = bf(hy_w_in[j]), bf(hy_w_out[j])
            hp = (hy_short_w[j], hy_short_b[j],
                  (hy_f_w1[j], hy_f_b1[j], hy_f_w2[j], hy_f_b2[j], hy_f_w3[j], hy_f_b3[j], hy_f_w4[j], hy_f_freq[j]),
                  hy_skip[j])
            z = _norm_mod_matmul(x, norm1_g[i], sh1, sc1, w_in, hy_b_in[j])
            y = _hyena_longconv_main(z, hp, tables)
            x = _out_proj(y, w_out, hy_b_out[j], x, g1)
            if ctx_out:
                zc = _norm_mod_matmul(xc, norm1_g[i], csh1, csc1, w_in, hy_b_in[j])
                yc = _hyena_longconv_ctx(zc, hp)
                xc = _out_proj(yc, w_out, hy_b_out[j], xc, cg1)
        elif kind == 1:
            w_in, w_out = bf(sc_w_in[j]), bf(sc_w_out[j])
            z = _norm_mod_matmul(x, norm1_g[i], sh1, sc1, w_in, zeros(3 * d))
            x = _out_proj(_sc_gate(z, sc_conv_w[j]), w_out, zeros(d), x, g1)
            if ctx_out:
                zc = _norm_mod_matmul(xc, norm1_g[i], csh1, csc1, w_in, zeros(3 * d))
                xc = _out_proj(_sc_gate(zc, sc_conv_w[j]), w_out, zeros(d), xc, cg1)
        else:
            w_qkv, w_o = bf(at_w_qkv[j]), bf(at_w_o[j])
            qkv_cols = w_qkv.shape[1]
            qkv = _norm_mod_matmul(x, norm1_g[i], sh1, sc1, w_qkv, zeros(qkv_cols))
            cosf, sinf = _rope_tables(l, dh)
            q, k, vt = _attn_prep(qkv, cosf, sinf, at_q_g[j], at_k_g[j], n_heads, n_kv, dh)
            if ctx_out:
                qkv_c = _norm_mod_matmul(xc, norm1_g[i], csh1, csc1, w_qkv, zeros(qkv_cols))
                qc, kc, vtc = _attn_prep(qkv_c, jnp.ones((lc, dh), F32), jnp.zeros((lc, dh), F32),
                                         at_q_g[j], at_k_g[j], n_heads, n_kv, dh)
                oc = _flash_attention(qc, kc, vtc, group, dh)
                xc_next = _out_proj(oc, w_o, zeros(d), xc, cg1)
            else:
                kv_c = _norm_mod_matmul(xc, norm1_g[i], csh1, csc1, w_qkv[:, n_heads * dh:], zeros(qkv_cols - d))
                kc, vtc = _attn_prep(kv_c, jnp.ones((lc, dh), F32), jnp.zeros((lc, dh), F32),
                                     at_q_g[j], at_k_g[j], 0, n_kv, dh)
            k_all = jnp.concatenate([k, kc], axis=2)
            vt_all = jnp.concatenate([vt, vtc], axis=3)
            o = _flash_attention(q, k_all, vt_all, group, dh)
            x = _out_proj(o, w_o, zeros(d), x, g1)
            if ctx_out:
                xc = xc_next
        x = _ffn(x, norm2_g[i], sh2, sc2, g2, wg, wu, wd)
        if ctx_out:
            xc = _ffn(xc, norm2_g[i], csh2, csc2, cg2, wg, wu, wd)
    return _final_norm(x, final_g)
```

```python
import functools
import math

import jax
import jax.numpy as jnp
from jax import lax
from jax.experimental import pallas as pl
from jax.experimental.pallas import tpu as pltpu

F32 = jnp.float32
BF16 = jnp.bfloat16
NORM_EPS = 1e-6
GRID_W = 64
ROPE_THETA = 10000.0
HYENA_FAST_DECAY = 0.3
HYENA_SLOW_DECAY = 1.5
HYENA_TARGET = 1e-2
N_MOD = 6
N_MIXERS = 3
FFT_N2 = 256
LANES = 128
SUBLANES = 8
HIGHEST = lax.Precision.HIGHEST


def _cp(sem, vmem_mb=None):
    kw = dict(dimension_semantics=sem)
    if vmem_mb is not None:
        kw["vmem_limit_bytes"] = vmem_mb << 20
    return pltpu.CompilerParams(**kw)


def _tile(n, pref):
    if n <= pref:
        return n
    t = pref
    while n % t:
        t //= 2
    return t


def _bdot(a, b):
    return jnp.dot(a, b, preferred_element_type=F32)


def _adaln_kernel(c_ref, w_ref, b_ref, o_ref):
    c = c_ref[...]
    s = (c * jax.nn.sigmoid(c)).astype(BF16)
    o_ref[...] = _bdot(s, w_ref[...].astype(BF16)) + b_ref[...]


def _adaln_all(cond8, w_mod, b_mod):
    depth, d, n = w_mod.shape
    tn = _tile(n, 1024)
    return pl.pallas_call(
        _adaln_kernel,
        grid=(depth, n // tn),
        in_specs=[
            pl.BlockSpec((SUBLANES, d), lambda l, j: (0, 0)),
            pl.BlockSpec((None, d, tn), lambda l, j: (l, 0, j)),
            pl.BlockSpec((None, 1, tn), lambda l, j: (l, 0, j)),
        ],
        out_specs=pl.BlockSpec((None, SUBLANES, tn), lambda l, j: (l, 0, j)),
        out_shape=jax.ShapeDtypeStruct((depth, SUBLANES, n), F32),
        compiler_params=_cp(("parallel", "parallel"), 40),
        name="adaln",
    )(cond8, w_mod, b_mod.reshape(depth, 1, n))


def _norm_mod(x, g, sh, sc):
    rs = lax.rsqrt(jnp.mean(x * x, axis=-1, keepdims=True) + NORM_EPS)
    return ((x * rs) * g) * (1.0 + sc) + sh


def _nm_kernel(x_ref, g_ref, sh_ref, sc_ref, w_ref, b_ref, o_ref, a_scr):
    @pl.when(pl.program_id(2) == 0)
    def _():
        a_scr[...] = _norm_mod(x_ref[...], g_ref[...], sh_ref[...], sc_ref[...]).astype(BF16)

    o_ref[...] = (_bdot(a_scr[...], w_ref[...]) + b_ref[...]).astype(o_ref.dtype)


def _norm_mod_matmul(x, g, sh, sc, w, bias, tm_pref=512, tn_pref=2048):
    bsz, l, d = x.shape
    n = w.shape[1]
    tm, tn = _tile(l, tm_pref), _tile(n, tn_pref)
    return pl.pallas_call(
        _nm_kernel,
        grid=(bsz, l // tm, n // tn),
        in_specs=[
            pl.BlockSpec((None, tm, d), lambda b, i, j: (b, i, 0)),
            pl.BlockSpec((1, d), lambda b, i, j: (0, 0)),
            pl.BlockSpec((None, 1, d), lambda b, i, j: (b, 0, 0)),
            pl.BlockSpec((None, 1, d), lambda b, i, j: (b, 0, 0)),
            pl.BlockSpec((d, tn), lambda b, i, j: (0, j)),
            pl.BlockSpec((1, tn), lambda b, i, j: (0, j)),
        ],
        out_specs=pl.BlockSpec((None, tm, tn), lambda b, i, j: (b, i, j)),
        out_shape=jax.ShapeDtypeStruct((bsz, l, n), F32),
        scratch_shapes=[pltpu.VMEM((tm, d), BF16)],
        compiler_params=_cp(("parallel", "parallel", "arbitrary"), 52),
        name="norm_mod_matmul",
    )(x, g.reshape(1, d), sh, sc, w, bias.reshape(1, n))


def _ffn_kernel(x_ref, g_ref, sh_ref, sc_ref, gate_ref, wg_ref, wu_ref, wd_ref, o_ref,
                a_scr, acc_scr, ha_scr, hb_scr):
    f = pl.program_id(2)
    nf = pl.num_programs(2) - 1

    def up_phase(h_w):
        a = a_scr[...]
        hg = _bdot(a, wg_ref[...])
        hu = _bdot(a, wu_ref[...])
        h_w[...] = ((hg * jax.nn.sigmoid(hg)) * hu).astype(BF16)

    def down_phase(h_r):
        acc_scr[...] += _bdot(h_r[...], wd_ref[...])

    @pl.when(f == 0)
    def _():
        a_scr[...] = _norm_mod(x_ref[...], g_ref[...], sh_ref[...], sc_ref[...]).astype(BF16)
        acc_scr[...] = jnp.zeros_like(acc_scr)
        up_phase(ha_scr)

    middle = (f > 0) & (f < nf)

    @pl.when(middle & (f % 2 == 1))
    def _():
        up_phase(hb_scr)
        down_phase(ha_scr)

    @pl.when(middle & (f % 2 == 0))
    def _():
        up_phase(ha_scr)
        down_phase(hb_scr)

    def drain(h_r):
        down_phase(h_r)
        o_ref[...] = x_ref[...] + gate_ref[...] * acc_scr[...]

    @pl.when((f == nf) & (f % 2 == 1))
    def _():
        drain(ha_scr)

    @pl.when((f == nf) & (f % 2 == 0))
    def _():
        drain(hb_scr)


def _ffn(x, g, sh, sc, gate, wg, wu, wd, tm_pref=512, tf_pref=512):
    bsz, l, d = x.shape
    fdim = wg.shape[1]
    tm, tf = _tile(l, tm_pref), _tile(fdim, tf_pref)
    nf = fdim // tf
    return pl.pallas_call(
        _ffn_kernel,
        grid=(bsz, l // tm, nf + 1),
        in_specs=[
            pl.BlockSpec((None, tm, d), lambda b, i, f: (b, i, 0)),
            pl.BlockSpec((1, d), lambda b, i, f: (0, 0)),
            pl.BlockSpec((None, 1, d), lambda b, i, f: (b, 0, 0)),
            pl.BlockSpec((None, 1, d), lambda b, i, f: (b, 0, 0)),
            pl.BlockSpec((None, 1, d), lambda b, i, f: (b, 0, 0)),
            pl.BlockSpec((d, tf), lambda b, i, f: (0, jnp.minimum(f, nf - 1))),
            pl.BlockSpec((d, tf), lambda b, i, f: (0, jnp.minimum(f, nf - 1))),
            pl.BlockSpec((tf, d), lambda b, i, f: (jnp.maximum(f - 1, 0), 0)),
        ],
        out_specs=pl.BlockSpec((None, tm, d), lambda b, i, f: (b, i, 0)),
        out_shape=jax.ShapeDtypeStruct((bsz, l, d), F32),
        scratch_shapes=[pltpu.VMEM((tm, d), BF16), pltpu.VMEM((tm, d), F32),
                        pltpu.VMEM((tm, tf), BF16), pltpu.VMEM((tm, tf), BF16)],
        compiler_params=_cp(("parallel", "parallel", "arbitrary"), 48),
        name="ffn",
    )(x, g.reshape(1, d), sh, sc, gate, wg, wu, wd)


def _out_kernel(y_ref, w_ref, b_ref, x_ref, gate_ref, o_ref):
    o_ref[...] = x_ref[...] + gate_ref[...] * (_bdot(y_ref[...].astype(BF16), w_ref[...]) + b_ref[...])


def _out_proj(y, w, bias, x, gate, tm_pref=512, tn_pref=1024):
    bsz, l, d = x.shape
    k = w.shape[0]
    tm, tn = _tile(l, tm_pref), _tile(d, tn_pref)
    return pl.pallas_call(
        _out_kernel,
        grid=(bsz, l // tm, d // tn),
        in_specs=[
            pl.BlockSpec((None, tm, k), lambda b, i, j: (b, i, 0)),
            pl.BlockSpec((k, tn), lambda b, i, j: (0, j)),
            pl.BlockSpec((1, tn), lambda b, i, j: (0, j)),
            pl.BlockSpec((None, tm, tn), lambda b, i, j: (b, i, j)),
            pl.BlockSpec((None, 1, tn), lambda b, i, j: (b, 0, j)),
        ],
        out_specs=pl.BlockSpec((None, tm, tn), lambda b, i, j: (b, i, j)),
        out_shape=jax.ShapeDtypeStruct((bsz, l, d), F32),
        compiler_params=_cp(("parallel", "parallel", "parallel"), 40),
        name="out_proj",
    )(y, w, bias.reshape(1, d), x, gate)


def _final_norm_kernel(x_ref, g_ref, o_ref):
    x = x_ref[...]
    o_ref[...] = (x * lax.rsqrt(jnp.mean(x * x, axis=-1, keepdims=True) + NORM_EPS)) * g_ref[...]


def _final_norm(x, g, tm_pref=512):
    bsz, l, d = x.shape
    tm = _tile(l, tm_pref)
    return pl.pallas_call(
        _final_norm_kernel,
        grid=(bsz, l // tm),
        in_specs=[pl.BlockSpec((None, tm, d), lambda b, i: (b, i, 0)), pl.BlockSpec((1, d), lambda b, i: (0, 0))],
        out_specs=pl.BlockSpec((None, tm, d), lambda b, i: (b, i, 0)),
        out_shape=jax.ShapeDtypeStruct((bsz, l, d), F32),
        compiler_params=_cp(("parallel", "parallel")),
        name="final_norm",
    )(x, g.reshape(1, d))


def _conv3(x, prev_row, next_row, w):
    tm = x.shape[0]
    row = lax.broadcasted_iota(jnp.int32, x.shape, 0)
    up = jnp.where(row == 0, prev_row, pltpu.roll(x, 1, 0))
    dn = jnp.where(row == tm - 1, next_row, pltpu.roll(x, tm - 1, 0))
    return up * w[0:1] + x * w[1:2] + dn * w[2:3]


def _halo_rows(p_ref, n_ref):
    i = pl.program_id(1)
    last = pl.num_programs(1) - 1
    prev_row = jnp.where(i == 0, 0.0, p_ref[SUBLANES - 1:SUBLANES, :])
    next_row = jnp.where(i == last, 0.0, n_ref[0:1, :])
    return prev_row, next_row


def _hy_gate_kernel(m0, p0, n0, m1, p1, n1, m2, p2, n2, w0, w1, w2, b0, b1, b2, vg_ref, x0_ref):
    def comp(m, p, n, w, b):
        pr, nx = _halo_rows(p, n)
        return _conv3(m[...], pr, nx, w[...]) + b[...]

    u0 = comp(m0, p0, n0, w0, b0)
    u1 = comp(m1, p1, n1, w1, b1)
    u2 = comp(m2, p2, n2, w2, b2)
    vg_ref[...] = u2 * u1
    x0_ref[...] = u0


def _sc_gate_kernel(m0, m1, p1, n1, m2, p2, n2, w_ref, o_ref):
    i = pl.program_id(1)
    last = pl.num_programs(1) - 1
    prev_row = jnp.where(i == 0, 0.0, p1[SUBLANES - 1:SUBLANES, :] * p2[SUBLANES - 1:SUBLANES, :])
    next_row = jnp.where(i == last, 0.0, n1[0:1, :] * n2[0:1, :])
    o_ref[...] = (m0[...] * _conv3(m1[...] * m2[...], prev_row, next_row, w_ref[...])).astype(o_ref.dtype)


def _halo_specs(tm, tc, l, col):
    r = tm // SUBLANES
    nblk = l // SUBLANES
    return [
        pl.BlockSpec((None, tm, tc), lambda b, i, j: (b, i, col + j)),
        pl.BlockSpec((None, SUBLANES, tc), lambda b, i, j: (b, jnp.maximum(i * r - 1, 0), col + j)),
        pl.BlockSpec((None, SUBLANES, tc), lambda b, i, j: (b, jnp.minimum((i + 1) * r, nblk - 1), col + j)),
    ]


def _hy_gate(z, short_w, short_b, tc_pref=1024):
    bsz, l, d3 = z.shape
    d = d3 // 3
    tm = _tile(l, 256)
    tc = _tile(d, tc_pref)
    nd = d // tc
    in_specs = []
    for comp in range(3):
        in_specs += _halo_specs(tm, tc, l, comp * nd)
    for comp in range(3):
        in_specs.append(pl.BlockSpec((3, tc), lambda b, i, j, c=comp: (0, c * nd + j)))
    for comp in range(3):
        in_specs.append(pl.BlockSpec((1, tc), lambda b, i, j, c=comp: (0, c * nd + j)))
    ospec = pl.BlockSpec((None, tm, tc), lambda b, i, j: (b, i, j))
    return pl.pallas_call(
        _hy_gate_kernel,
        grid=(bsz, l // tm, nd),
        in_specs=in_specs,
        out_specs=[ospec, ospec],
        out_shape=[jax.ShapeDtypeStruct((bsz, l, d), F32)] * 2,
        compiler_params=_cp(("parallel", "parallel", "parallel"), 40),
        name="hyena_gate",
    )(*([z] * 9), short_w, short_w, short_w, *([short_b.reshape(1, d3)] * 3))


def _sc_gate(z, conv_w, tc_pref=1024):
    bsz, l, d3 = z.shape
    d = d3 // 3
    tm = _tile(l, 256)
    tc = _tile(d, tc_pref)
    nd = d // tc
    in_specs = [_halo_specs(tm, tc, l, 0)[0]] + _halo_specs(tm, tc, l, nd) + _halo_specs(tm, tc, l, 2 * nd)
    in_specs.append(pl.BlockSpec((3, tc), lambda b, i, j: (0, j)))
    return pl.pallas_call(
        _sc_gate_kernel,
        grid=(bsz, l // tm, nd),
        in_specs=in_specs,
        out_specs=pl.BlockSpec((None, tm, tc), lambda b, i, j: (b, i, j)),
        out_shape=jax.ShapeDtypeStruct((bsz, l, d), BF16),
        compiler_params=_cp(("parallel", "parallel", "parallel"), 40),
        name="shortconv_gate",
    )(*([z] * 7), conv_w)


def _filter_features(l):
    emb = 33
    bands = (emb - 1) // 2
    rows = jnp.arange(2 * l, dtype=jnp.int32)
    pos = jnp.where(rows < l, rows, 2 * l - rows).astype(F32)[:, None]
    t = pos / (l - 1)
    omega = 2.0 * math.pi * pos / l
    fb = jnp.linspace(1e-4, bands - 1, bands, dtype=F32)[None, :]
    z = jnp.concatenate([t, jnp.cos(fb * omega), -jnp.sin(fb * omega)], axis=-1)
    return jnp.pad(z, ((0, 0), (0, LANES - emb)))


def _filt_kernel(z_ref, w1, b1, w2, b2, w3, b3, fr, w4h, w4l, dab, o_ref):
    z = z_ref[...]

    def hdot(a, b):
        return jnp.dot(a, b, precision=HIGHEST, preferred_element_type=F32)

    f = fr[...]
    h = jnp.sin(f * (hdot(z, w1[...]) + b1[...]))
    h = jnp.sin(f * (hdot(h, w2[...]) + b2[...]))
    h = jnp.sin(f * (hdot(h, w3[...]) + b3[...]))
    h_hi = h.astype(BF16)
    h_lo = (h - h_hi.astype(F32)).astype(BF16)
    hw = _bdot(h_hi, w4h[...]) + _bdot(h_lo, w4h[...]) + _bdot(h_hi, w4l[...])
    k = hw * jnp.exp(-(z[:, 0:1] * dab[...]))
    row = lax.broadcasted_iota(jnp.int32, k.shape, 0)
    first_bwd = pl.program_id(0) * 2 == pl.num_programs(0)
    o_ref[...] = jnp.where((row == 0) & first_bwd, 0.0, k)


def _hyena_filter(l, p, d, tc_pref=2048):
    w1, b1, w2, b2, w3, b3, w4, freq = p
    order = w2.shape[0]
    zf = _filter_features(l)

    def pad2(a, rows, cols):
        return jnp.pad(a, ((0, rows - a.shape[0]), (0, cols - a.shape[1])))

    w1p = pad2(w1, LANES, LANES)
    w2p = pad2(w2, LANES, LANES)
    w3p = pad2(w3, LANES, LANES)
    b1p = pad2(b1.reshape(1, order), 1, LANES)
    b2p = pad2(b2.reshape(1, order), 1, LANES)
    b3p = pad2(b3.reshape(1, order), 1, LANES)
    frp = pad2(freq.reshape(1, order), 1, LANES)
    w4p = pad2(w4, LANES, 2 * d)
    w4h = w4p.astype(BF16)
    w4l = (w4p - w4h.astype(F32)).astype(BF16)
    max_decay = math.log(HYENA_TARGET) / HYENA_FAST_DECAY
    min_decay = math.log(HYENA_TARGET) / HYENA_SLOW_DECAY
    dab = jnp.abs(jnp.linspace(min_decay, max_decay, d, dtype=F32)).reshape(1, d)
    tc = _tile(d, tc_pref)
    nd = d // tc
    r = _tile(l, 512)
    nb = 2 * l // r
    small = pl.BlockSpec((LANES, LANES), lambda i, j: (0, 0))
    vec = pl.BlockSpec((1, LANES), lambda i, j: (0, 0))
    bank = pl.BlockSpec((LANES, tc), lambda i, j: (0, (i * 2 // nb) * nd + j))
    return pl.pallas_call(
        _filt_kernel,
        grid=(nb, nd),
        in_specs=[pl.BlockSpec((r, LANES), lambda i, j: (i, 0)),
                  small, vec, small, vec, small, vec, vec, bank, bank, pl.BlockSpec((1, tc), lambda i, j: (0, j))],
        out_specs=pl.BlockSpec((r, tc), lambda i, j: (i, j)),
        out_shape=jax.ShapeDtypeStruct((2 * l, d), F32),
        compiler_params=_cp(("parallel", "parallel"), 40),
        name="hyena_filter",
    )(zf, w1p, b1p, w2p, b2p, w3p, b3p, frp, w4h, w4l, dab)


def _cs(m, n):
    ang = (m % n).astype(F32) * (2.0 * math.pi / n)
    return jnp.cos(ang), jnp.sin(ang)


def _blk(a, b, c, d):
    return jnp.concatenate([jnp.concatenate([a, b], axis=-1), jnp.concatenate([c, d], axis=-1)], axis=-2)


def _fft_tables(l):
    n = 2 * l
    n1 = n // FFT_N2
    a_cnt = n1 // 2
    k1 = jnp.arange(n1, dtype=jnp.int32)
    b = jnp.arange(FFT_N2, dtype=jnp.int32)
    a_full = jnp.arange(n1, dtype=jnp.int32)
    pos = FFT_N2 * a_full[None, :] + b[:, None]
    c, s = _cs(k1[None, :, None] * pos[:, None, :], n)
    cd, sd = c[:, :, :a_cnt], s[:, :, :a_cnt]
    g_data = _blk(cd, sd, -sd, cd).astype(BF16)
    g_filt = jnp.concatenate([c, -s], axis=-2).astype(BF16)
    ct, st = jnp.swapaxes(cd, 1, 2), jnp.swapaxes(sd, 1, 2)
    g_fin = _blk(ct, -st, st, ct).astype(BF16)
    c2, s2 = _cs(b[:, None] * b[None, :], FFT_N2)
    f2_fwd = _blk(c2, s2, -s2, c2).astype(BF16)
    f2_inv = _blk(c2, -s2, s2, c2).astype(BF16)
    return g_data, g_filt, g_fin, f2_fwd, f2_inv


def _per_b_matmul(x, g_ref):
    xt = pltpu.einshape("rjc->jrc", x)
    ys = [_bdot(g_ref[j], xt[j].astype(BF16)) for j in range(SUBLANES)]
    return pltpu.einshape("jmc->mjc", jnp.stack(ys, axis=0))


def _s1_kernel(x_ref, g_ref, o_ref):
    o_ref[...] = _per_b_matmul(x_ref[...], g_ref)


def _fft_stage1(x4, g, tc_pref=512):
    r, nt, sub, d = x4.shape
    m = g.shape[1]
    tc = _tile(d, tc_pref)
    return pl.pallas_call(
        _s1_kernel,
        grid=(nt, d // tc),
        in_specs=[pl.BlockSpec((r, None, sub, tc), lambda t, j: (0, t, 0, j)),
                  pl.BlockSpec((sub, m, r), lambda t, j: (t, 0, 0))],
        out_specs=pl.BlockSpec((m, None, sub, tc), lambda t, j: (0, t, 0, j)),
        out_shape=jax.ShapeDtypeStruct((m, nt, sub, d), F32),
        compiler_params=_cp(("parallel", "arbitrary"), 48),
        name="fft_stage1",
    )(x4, g)


def _fmid_kernel(y_ref, f_ref, o_ref, *, scale):
    y = jnp.concatenate([y_ref[0], y_ref[1]], axis=0).astype(BF16)
    o_ref[...] = _bdot(f_ref[...], y) * scale


def _fft_filter_spectrum(y1, f2_fwd, scale, tc_pref=1024):
    m, n2, d = y1.shape
    n1 = m // 2
    tc = _tile(d, tc_pref)
    y4 = y1.reshape(2, n1, n2, d)
    return pl.pallas_call(
        functools.partial(_fmid_kernel, scale=scale),
        grid=(n1, d // tc),
        in_specs=[pl.BlockSpec((2, None, n2, tc), lambda k, j: (0, k, 0, j)),
                  pl.BlockSpec((2 * n2, 2 * n2), lambda k, j: (0, 0))],
        out_specs=pl.BlockSpec((None, 2 * n2, tc), lambda k, j: (k, 0, j)),
        out_shape=jax.ShapeDtypeStruct((n1, 2 * n2, d), F32),
        compiler_params=_cp(("parallel", "parallel"), 40),
        name="fft_filter_stage2",
    )(y4, f2_fwd)


def _dmid_kernel(y_ref, k_ref, ff_ref, fi_ref, t_ref):
    n2 = y_ref.shape[1]
    y = jnp.concatenate([y_ref[0], y_ref[1]], axis=0).astype(BF16)
    s = _bdot(ff_ref[...], y)
    sr, si = s[:n2], s[n2:]
    kr, ki = k_ref[:n2, :], k_ref[n2:, :]
    p = jnp.concatenate([sr * kr - si * ki, sr * ki + si * kr], axis=0).astype(BF16)
    t = _bdot(fi_ref[...], p)
    t_ref[0] = t[:n2]
    t_ref[1] = t[n2:]


def _fft_data_mid(y1, kspec, f2_fwd, f2_inv, tc_pref=1024):
    m, n2, d = y1.shape
    n1 = m // 2
    tc = _tile(d, tc_pref)
    y4 = y1.reshape(2, n1, n2, d)
    tbl = pl.BlockSpec((2 * n2, 2 * n2), lambda k, j: (0, 0))
    yspec = pl.BlockSpec((2, None, n2, tc), lambda k, j: (0, k, 0, j))
    return pl.pallas_call(
        _dmid_kernel,
        grid=(n1, d // tc),
        in_specs=[yspec, pl.BlockSpec((None, 2 * n2, tc), lambda k, j: (k, 0, j)), tbl, tbl],
        out_specs=yspec,
        out_shape=jax.ShapeDtypeStruct((2, n1, n2, d), F32),
        compiler_params=_cp(("parallel", "parallel"), 48),
        name="fft_data_mid",
    )(y4, kspec, f2_fwd, f2_inv)


def _fin_kernel(t_ref, g_ref, vg_ref, x0_ref, skip_ref, o_ref):
    conv = _per_b_matmul(t_ref[...], g_ref)
    o_ref[...] = (conv + vg_ref[...] * skip_ref[...]) * x0_ref[...]


def _fft_final(t4, g_fin, vg, x0, skip, tc_pref=512):
    _, n1, n2, d = t4.shape
    bsz, l, _ = vg.shape
    nt = n2 // SUBLANES
    rows_in = 2 * n1
    rows_out = bsz * l // n2
    tc = _tile(d, tc_pref)
    vspec = pl.BlockSpec((rows_out, None, SUBLANES, tc), lambda t, j: (0, t, 0, j))
    out = pl.pallas_call(
        _fin_kernel,
        grid=(nt, d // tc),
        in_specs=[pl.BlockSpec((rows_in, None, SUBLANES, tc), lambda t, j: (0, t, 0, j)),
                  pl.BlockSpec((SUBLANES, rows_out, rows_in), lambda t, j: (t, 0, 0)),
                  vspec, vspec, pl.BlockSpec((1, 1, tc), lambda t, j: (0, 0, j))],
        out_specs=vspec,
        out_shape=jax.ShapeDtypeStruct((rows_out, nt, SUBLANES, d), F32),
        compiler_params=_cp(("parallel", "arbitrary"), 48),
        name="fft_final",
    )(t4.reshape(rows_in, nt, SUBLANES, d), g_fin, vg.reshape(rows_out, nt, SUBLANES, d),
      x0.reshape(rows_out, nt, SUBLANES, d), skip.reshape(1, 1, d))
    return out.reshape(bsz, l, d)


def _hyena_longconv_main(z, hp, tables):
    (short_w, short_b, fparams, skip) = hp
    g_data, g_filt, g_fin, f2_fwd, f2_inv = tables
    bsz, l, d3 = z.shape
    d = d3 // 3
    n = 2 * l
    n1 = n // FFT_N2
    nt = FFT_N2 // SUBLANES
    vg, x0 = _hy_gate(z, short_w, short_b)
    kt = _hyena_filter(l, fparams, d)
    ky1 = _fft_stage1(kt.reshape(n1, nt, SUBLANES, d), g_filt)
    kspec = _fft_filter_spectrum(ky1.reshape(2 * n1, FFT_N2, d), f2_fwd, 1.0 / n)
    y1 = _fft_stage1(vg.reshape(bsz * l // FFT_N2, nt, SUBLANES, d), g_data)
    t4 = _fft_data_mid(y1.reshape(2 * n1, FFT_N2, d), kspec, f2_fwd, f2_inv)
    return _fft_final(t4, g_fin, vg, x0, skip)


def _ctxconv_kernel(vg_ref, x0_ref, k_ref, skip_ref, fd_ref, ff_ref, fi_ref, o_ref, *, scale):
    lc = vg_ref.shape[1]
    n = 2 * lc
    z = jnp.concatenate([vg_ref[0], vg_ref[1]], axis=0).astype(BF16)
    s = _bdot(fd_ref[...], z)
    ks = _bdot(ff_ref[...], k_ref[...].astype(BF16)) * scale
    sr, si = s[:n], s[n:]
    kr, ki = ks[:n], ks[n:]
    p = jnp.concatenate([sr * kr - si * ki, sr * ki + si * kr], axis=0).astype(BF16)
    y = _bdot(fi_ref[...], p)
    sk = skip_ref[...]
    o_ref[0] = (y[:lc] + vg_ref[0] * sk) * x0_ref[0]
    o_ref[1] = (y[lc:] + vg_ref[1] * sk) * x0_ref[1]


def _hyena_longconv_ctx(z, hp, tc_pref=512):
    (short_w, short_b, fparams, skip) = hp
    bsz, lc, d3 = z.shape
    d = d3 // 3
    n = 2 * lc
    vg, x0 = _hy_gate(z, short_w, short_b)
    kt = _hyena_filter(lc, fparams, d)
    kf = jnp.arange(n, dtype=jnp.int32)
    c, s = _cs(kf[:, None] * kf[None, :], n)
    cd, sd = c[:, :lc], s[:, :lc]
    fd = _blk(cd, sd, -sd, cd).astype(BF16)
    ff = jnp.concatenate([c, -s], axis=0).astype(BF16)
    fi = _blk(cd.T, -sd.T, sd.T, cd.T).astype(BF16)
    tc = _tile(d, tc_pref)
    xspec = pl.BlockSpec((bsz, lc, tc), lambda j: (0, 0, j))
    full = lambda a: pl.BlockSpec(a.shape, lambda j: (0, 0))
    return pl.pallas_call(
        functools.partial(_ctxconv_kernel, scale=1.0 / n),
        grid=(d // tc,),
        in_specs=[xspec, xspec, pl.BlockSpec((n, tc), lambda j: (0, j)), pl.BlockSpec((1, tc), lambda j: (0, j)),
                  full(fd), full(ff), full(fi)],
        out_specs=xspec,
        out_shape=jax.ShapeDtypeStruct((bsz, lc, d), F32),
        compiler_params=_cp(("parallel",), 40),
        name="hyena_ctx_conv",
    )(vg, x0, kt, skip.reshape(1, d), fd, ff, fi)


def _rope_tables(l, dh):
    rows = l // GRID_W
    d_axis = dh // 2
    inv_freq = ROPE_THETA ** (-jnp.arange(0, d_axis, 2, dtype=F32) / d_axis)
    ang_r = jnp.arange(rows, dtype=F32)[:, None] * inv_freq
    ang_c = jnp.arange(GRID_W, dtype=F32)[:, None] * inv_freq
    half = d_axis // 2
    ang = jnp.concatenate([jnp.broadcast_to(ang_r[:, None, :], (rows, GRID_W, half)),
                           jnp.broadcast_to(ang_c[None, :, :], (rows, GRID_W, half))], axis=-1)
    ang = ang.reshape(rows * GRID_W, d_axis)
    cos, sin = jnp.cos(ang), jnp.sin(ang)
    cosf = jnp.repeat(cos, 2, axis=-1)
    sinf = jnp.stack([-sin, sin], axis=-1).reshape(rows * GRID_W, dh)
    return cosf, sinf


def _prep_kernel(x_ref, cos_ref, sin_ref, qg_ref, kg_ref, *outs, nq, nkv, dh, qscale):
    cosf = cos_ref[...]
    sinf = sin_ref[...]
    even = (lax.broadcasted_iota(jnp.int32, cosf.shape, 1) % 2) == 0

    def norm_rope(xh, g):
        y = (xh * lax.rsqrt(jnp.mean(xh * xh, axis=-1, keepdims=True) + NORM_EPS)) * g
        sw = jnp.where(even, pltpu.roll(y, dh - 1, 1), pltpu.roll(y, 1, 1))
        return y * cosf + sw * sinf

    if nq:
        q_ref, k_ref, vt_ref = outs
        for h in range(nq):
            q_ref[:, h * dh:(h + 1) * dh] = (norm_rope(x_ref[:, h * dh:(h + 1) * dh], qg_ref[...]) * qscale).astype(BF16)
    else:
        k_ref, vt_ref = outs
    for j in range(nkv):
        c0 = (nq + j) * dh
        k_ref[j] = norm_rope(x_ref[:, c0:c0 + dh], kg_ref[...]).astype(BF16)
    for j in range(nkv):
        c0 = (nq + nkv + j) * dh
        vt_ref[j] = x_ref[:, c0:c0 + dh].T.astype(BF16)


def _attn_prep(x, cosf, sinf, q_g, k_g, nq, nkv, dh, tm_pref=256):
    bsz, l, cols = x.shape
    tm = _tile(l, tm_pref)
    out_specs = [pl.BlockSpec((None, nkv, tm, dh), lambda b, i: (b, 0, i, 0)),
                 pl.BlockSpec((None, nkv, dh, tm), lambda b, i: (b, 0, 0, i))]
    out_shape = [jax.ShapeDtypeStruct((bsz, nkv, l, dh), BF16), jax.ShapeDtypeStruct((bsz, nkv, dh, l), BF16)]
    if nq:
        out_specs = [pl.BlockSpec((None, tm, nq * dh), lambda b, i: (b, i, 0))] + out_specs
        out_shape = [jax.ShapeDtypeStruct((bsz, l, nq * dh), BF16)] + out_shape
    tab = pl.BlockSpec((tm, dh), lambda b, i: (i, 0))
    gain = pl.BlockSpec((1, dh), lambda b, i: (0, 0))
    return pl.pallas_call(
        functools.partial(_prep_kernel, nq=nq, nkv=nkv, dh=dh, qscale=dh ** -0.5 * math.log2(math.e)),
        grid=(bsz, l // tm),
        in_specs=[pl.BlockSpec((None, tm, cols), lambda b, i: (b, i, 0)), tab, tab, gain, gain],
        out_specs=out_specs,
        out_shape=out_shape,
        compiler_params=_cp(("parallel", "parallel"), 40),
        name="attn_prep",
    )(x, cosf, sinf, q_g.reshape(1, dh), k_g.reshape(1, dh))


def _flash_kernel(q_ref, k_ref, vt_ref, o_ref, sa_scr, sb_scr, ca_scr, cb_scr, m_scr, l_scr, acc_scr, *, group, dh):
    kk = pl.program_id(3)
    nk = pl.num_programs(3) - 1

    def score_phase(s_w, c_w):
        k = k_ref[...]
        for h in range(group):
            s = lax.dot_general(k, q_ref[:, h * dh:(h + 1) * dh], (((1,), (1,)), ((), ())),
                                preferred_element_type=F32)
            s_w[h] = s
            c_w[h] = jnp.max(s, axis=0, keepdims=True)

    def softmax_phase(s_r, c_r):
        vt = vt_ref[...]
        for h in range(group):
            m_old = m_scr[h]
            m_new = jnp.maximum(m_old, c_r[h])
            alpha = jnp.exp2(m_old - m_new)
            p = jnp.exp2(s_r[h] - m_new)
            l_scr[h] = alpha * l_scr[h] + jnp.sum(p, axis=0, keepdims=True)
            m_scr[h] = m_new
            acc_scr[h] = alpha * acc_scr[h] + _bdot(vt, p.astype(BF16))

    @pl.when(kk == 0)
    def _():
        m_scr[...] = jnp.full_like(m_scr, -jnp.inf)
        l_scr[...] = jnp.zeros_like(l_scr)
        acc_scr[...] = jnp.zeros_like(acc_scr)
        score_phase(sa_scr, ca_scr)

    middle = (kk > 0) & (kk < nk)

    @pl.when(middle & (kk % 2 == 1))
    def _():
        score_phase(sb_scr, cb_scr)
        softmax_phase(sa_scr, ca_scr)

    @pl.when(middle & (kk % 2 == 0))
    def _():
        score_phase(sa_scr, ca_scr)
        softmax_phase(sb_scr, cb_scr)

    def drain(s_r, c_r):
        softmax_phase(s_r, c_r)
        for h in range(group):
            o = acc_scr[h] / l_scr[h]
            o_ref[:, h * dh:(h + 1) * dh] = o.T.astype(o_ref.dtype)

    @pl.when((kk == nk) & (kk % 2 == 1))
    def _():
        drain(sa_scr, ca_scr)

    @pl.when((kk == nk) & (kk % 2 == 0))
    def _():
        drain(sb_scr, cb_scr)


def _key_tile(lk, pref):
    best = LANES
    t = LANES
    while t <= min(lk, pref):
        if lk % t == 0:
            best = t
        t += LANES
    return best


def _flash_attention(q, k, vt, group, dh, tq_pref=512, tk_pref=1280):
    bsz, l, hd = q.shape
    nkv, lk = k.shape[1], k.shape[2]
    tq = _tile(l, tq_pref)
    tk = _key_tile(lk, tk_pref)
    gw = group * dh
    nk = lk // tk
    stat = pltpu.VMEM((group, 1, tq), F32)
    score = pltpu.VMEM((group, tk, tq), F32)
    return pl.pallas_call(
        functools.partial(_flash_kernel, group=group, dh=dh),
        grid=(bsz, nkv, l // tq, nk + 1),
        in_specs=[
            pl.BlockSpec((None, tq, gw), lambda b, g, i, kk: (b, i, g)),
            pl.BlockSpec((None, None, tk, dh), lambda b, g, i, kk: (b, g, jnp.minimum(kk, nk - 1), 0)),
            pl.BlockSpec((None, None, dh, tk), lambda b, g, i, kk: (b, g, 0, jnp.maximum(kk - 1, 0))),
        ],
        out_specs=pl.BlockSpec((None, tq, gw), lambda b, g, i, kk: (b, i, g)),
        out_shape=jax.ShapeDtypeStruct((bsz, l, hd), BF16),
        scratch_shapes=[score, score, stat, stat, stat, stat, pltpu.VMEM((group, dh, tq), F32)],
        compiler_params=_cp(("parallel", "parallel", "parallel", "arbitrary"), 56),
        name="flash_attention",
    )(q, k, vt)


def kernel(x, c, ctx, c_ctx, norm1_g, norm2_g, w_mod, b_mod, ffn_w_gate, ffn_w_up, ffn_w_down, hy_w_in, hy_b_in,
           hy_short_w, hy_short_b, hy_f_w1, hy_f_b1, hy_f_w2, hy_f_b2, hy_f_w3, hy_f_b3, hy_f_w4, hy_f_freq,
           hy_skip, hy_w_out, hy_b_out, sc_w_in, sc_conv_w, sc_w_out, at_w_qkv, at_q_g, at_k_g, at_w_o, final_g):
    bsz, l, d = x.shape
    lc = ctx.shape[1]
    depth = norm1_g.shape[0]
    dh = at_q_g.shape[-1]
    n_heads = d // dh
    n_kv = (at_w_qkv.shape[-1] - d) // (2 * dh)
    group = n_heads // n_kv
    assert bsz == 2, "the long convolution packs exactly two batch samples into one complex signal"
    assert l % FFT_N2 == 0 and l % GRID_W == 0

    attn_layers = [i for i in range(depth) if i % N_MIXERS == 2]
    last_ctx_reader = attn_layers[-1] if attn_layers else -1

    cond8 = jnp.zeros((SUBLANES, d), F32).at[:bsz].set(c).at[bsz].set(c_ctx)
    mods = _adaln_all(cond8, w_mod, b_mod)

    def mod_set(i, ctx_rows):
        m = mods[i]
        if ctx_rows:
            rows = jnp.broadcast_to(m[bsz:bsz + 1], (bsz, N_MOD * d))
        else:
            rows = m[:bsz]
        return [rows[:, None, k * d:(k + 1) * d] for k in range(N_MOD)]

    zeros = lambda n: jnp.zeros((n,), F32)
    bf = lambda w: w.astype(BF16)
    tables = None
    if any(i % N_MIXERS == 0 for i in range(depth)):
        tables = _fft_tables(l)

    xc = ctx
    for i in range(depth):
        kind, j = i % N_MIXERS, i // N_MIXERS
        ctx_in = i <= last_ctx_reader
        ctx_out = i < last_ctx_reader
        sh1, sc1, g1, sh2, sc2, g2 = mod_set(i, False)
        if ctx_in:
            csh1, csc1, cg1, csh2, csc2, cg2 = mod_set(i, True)
        wg, wu, wd = bf(ffn_w_gate[i]), bf(ffn_w_up[i]), bf(ffn_w_down[i])
        if kind == 0:
            w_in, w_out = bf(hy_w_in[j]), bf(hy_w_out[j])
            hp = (hy_short_w[j], hy_short_b[j],
                  (hy_f_w1[j], hy_f_b1[j], hy_f_w2[j], hy_f_b2[j], hy_f_w3[j], hy_f_b3[j], hy_f_w4[j], hy_f_freq[j]),
                  hy_skip[j])
            z = _norm_mod_matmul(x, norm1_g[i], sh1, sc1, w_in, hy_b_in[j])
            y = _hyena_longconv_main(z, hp, tables)
            x = _out_proj(y, w_out, hy_b_out[j], x, g1)
            if ctx_out:
                zc = _norm_mod_matmul(xc, norm1_g[i], csh1, csc1, w_in, hy_b_in[j])
                yc = _hyena_longconv_ctx(zc, hp)
                xc = _out_proj(yc, w_out, hy_b_out[j], xc, cg1)
        elif kind == 1:
            w_in, w_out = bf(sc_w_in[j]), bf(sc_w_out[j])
            z = _norm_mod_matmul(x, norm1_g[i], sh1, sc1, w_in, zeros(3 * d))
            x = _out_proj(_sc_gate(z, sc_conv_w[j]), w_out, zeros(d), x, g1)
            if ctx_out:
                zc = _norm_mod_matmul(xc, norm1_g[i], csh1, csc1, w_in, zeros(3 * d))
                xc = _out_proj(_sc_gate(zc, sc_conv_w[j]), w_out, zeros(d), xc, cg1)
        else:
            w_qkv, w_o = bf(at_w_qkv[j]), bf(at_w_o[j])
            qkv_cols = w_qkv.shape[1]
            qkv = _norm_mod_matmul(x, norm1_g[i], sh1, sc1, w_qkv, zeros(qkv_cols))
            cosf, sinf = _rope_tables(l, dh)
            q, k, vt = _attn_prep(qkv, cosf, sinf, at_q_g[j], at_k_g[j], n_heads, n_kv, dh)
            if ctx_out:
                qkv_c = _norm_mod_matmul(xc, norm1_g[i], csh1, csc1, w_qkv, zeros(qkv_cols))
                qc, kc, vtc = _attn_prep(qkv_c, jnp.ones((lc, dh), F32), jnp.zeros((lc, dh), F32),
                                         at_q_g[j], at_k_g[j], n_heads, n_kv, dh)
                oc = _flash_attention(qc, kc, vtc, group, dh)
                xc_next = _out_proj(oc, w_o, zeros(d), xc, cg1)
            else:
                kv_c = _norm_mod_matmul(xc, norm1_g[i], csh1, csc1, w_qkv[:, n_heads * dh:], zeros(qkv_cols - d))
                kc, vtc = _attn_prep(kv_c, jnp.ones((lc, dh), F32), jnp.zeros((lc, dh), F32),
                                     at_q_g[j], at_k_g[j], 0, n_kv, dh)
            k_all = jnp.concatenate([k, kc], axis=2)
            vt_all = jnp.concatenate([vt, vtc], axis=3)
            o = _flash_attention(q, k_all, vt_all, group, dh)
            x = _out_proj(o, w_o, zeros(d), x, g1)
            if ctx_out:
                xc = xc_next
        x = _ffn(x, norm2_g[i], sh2, sc2, g2, wg, wu, wd)
        if ctx_out:
            xc = _ffn(xc, norm2_g[i], csh2, csc2, cg2, wg, wu, wd)
    return _final_norm(x, final_g)
```

```python
import functools
import math

import jax
import jax.numpy as jnp
from jax import lax
from jax.experimental import pallas as pl
from jax.experimental.pallas import tpu as pltpu

F32 = jnp.float32
BF16 = jnp.bfloat16
NORM_EPS = 1e-6
GRID_W = 64
ROPE_THETA = 10000.0
HYENA_FAST_DECAY = 0.3
HYENA_SLOW_DECAY = 1.5
HYENA_TARGET = 1e-2
N_MOD = 6
N_MIXERS = 3
FFT_N2 = 256
LANES = 128
SUBLANES = 8
HALO = 16
HIGHEST = lax.Precision.HIGHEST


def _cp(sem, vmem_mb=None):
    kw = dict(dimension_semantics=sem)
    if vmem_mb is not None:
        kw["vmem_limit_bytes"] = vmem_mb << 20
    return pltpu.CompilerParams(**kw)


def _tile(n, pref):
    if n <= pref:
        return n
    t = pref
    while n % t:
        t //= 2
    return t


def _bdot(a, b):
    return jnp.dot(a, b, preferred_element_type=F32)


def _adaln_kernel(c_ref, w_ref, b_ref, o_ref):
    c = c_ref[...]
    s = (c * jax.nn.sigmoid(c)).astype(BF16)
    o_ref[...] = _bdot(s, w_ref[...].astype(BF16)) + b_ref[...]


def _adaln_all(cond8, w_mod, b_mod):
    depth, d, n = w_mod.shape
    tn = _tile(n, 1024)
    return pl.pallas_call(
        _adaln_kernel,
        grid=(depth, n // tn),
        in_specs=[
            pl.BlockSpec((SUBLANES, d), lambda l, j: (0, 0)),
            pl.BlockSpec((None, d, tn), lambda l, j: (l, 0, j)),
            pl.BlockSpec((None, 1, tn), lambda l, j: (l, 0, j)),
        ],
        out_specs=pl.BlockSpec((None, SUBLANES, tn), lambda l, j: (l, 0, j)),
        out_shape=jax.ShapeDtypeStruct((depth, SUBLANES, n), F32),
        compiler_params=_cp(("parallel", "parallel"), 40),
        name="adaln",
    )(cond8, w_mod, b_mod.reshape(depth, 1, n))


def _norm_mod(x, g, sh, sc):
    rs = lax.rsqrt(jnp.mean(x * x, axis=-1, keepdims=True) + NORM_EPS)
    return ((x * rs) * g) * (1.0 + sc) + sh


def _nm_kernel(x_ref, g_ref, sh_ref, sc_ref, w_ref, b_ref, o_ref, a_scr):
    @pl.when(pl.program_id(2) == 0)
    def _():
        a_scr[...] = _norm_mod(x_ref[...], g_ref[...], sh_ref[...], sc_ref[...]).astype(BF16)

    o_ref[...] = (_bdot(a_scr[...], w_ref[...]) + b_ref[...]).astype(o_ref.dtype)


def _norm_mod_matmul(x, g, sh, sc, w, bias, tm_pref=512, tn_pref=2048):
    bsz, l, d = x.shape
    n = w.shape[1]
    tm, tn = _tile(l, tm_pref), _tile(n, tn_pref)
    return pl.pallas_call(
        _nm_kernel,
        grid=(bsz, l // tm, n // tn),
        in_specs=[
            pl.BlockSpec((None, tm, d), lambda b, i, j: (b, i, 0)),
            pl.BlockSpec((1, d), lambda b, i, j: (0, 0)),
            pl.BlockSpec((None, 1, d), lambda b, i, j: (b, 0, 0)),
            pl.BlockSpec((None, 1, d), lambda b, i, j: (b, 0, 0)),
            pl.BlockSpec((d, tn), lambda b, i, j: (0, j)),
            pl.BlockSpec((1, tn), lambda b, i, j: (0, j)),
        ],
        out_specs=pl.BlockSpec((None, tm, tn), lambda b, i, j: (b, i, j)),
        out_shape=jax.ShapeDtypeStruct((bsz, l, n), BF16),
        scratch_shapes=[pltpu.VMEM((tm, d), BF16)],
        compiler_params=_cp(("parallel", "parallel", "arbitrary"), 52),
        name="norm_mod_matmul",
    )(x, g.reshape(1, d), sh, sc, w, bias.reshape(1, n))


def _ffn_kernel(x_ref, g_ref, sh_ref, sc_ref, gate_ref, wg_ref, wu_ref, wd_ref, o_ref, a_scr, acc_scr):
    f = pl.program_id(2)

    @pl.when(f == 0)
    def _():
        a_scr[...] = _norm_mod(x_ref[...], g_ref[...], sh_ref[...], sc_ref[...]).astype(BF16)
        acc_scr[...] = jnp.zeros_like(acc_scr)

    a = a_scr[...]
    hg = _bdot(a, wg_ref[...])
    hu = _bdot(a, wu_ref[...])
    h = (hg * jax.nn.sigmoid(hg)) * hu
    acc_scr[...] += _bdot(h.astype(BF16), wd_ref[...])

    @pl.when(f == pl.num_programs(2) - 1)
    def _():
        o_ref[...] = x_ref[...] + gate_ref[...] * acc_scr[...]


def _ffn(x, g, sh, sc, gate, wg, wu, wd, tm_pref=512, tf_pref=512):
    bsz, l, d = x.shape
    fdim = wg.shape[1]
    tm, tf = _tile(l, tm_pref), _tile(fdim, tf_pref)
    return pl.pallas_call(
        _ffn_kernel,
        grid=(bsz, l // tm, fdim // tf),
        in_specs=[
            pl.BlockSpec((None, tm, d), lambda b, i, f: (b, i, 0)),
            pl.BlockSpec((1, d), lambda b, i, f: (0, 0)),
            pl.BlockSpec((None, 1, d), lambda b, i, f: (b, 0, 0)),
            pl.BlockSpec((None, 1, d), lambda b, i, f: (b, 0, 0)),
            pl.BlockSpec((None, 1, d), lambda b, i, f: (b, 0, 0)),
            pl.BlockSpec((d, tf), lambda b, i, f: (0, f)),
            pl.BlockSpec((d, tf), lambda b, i, f: (0, f)),
            pl.BlockSpec((tf, d), lambda b, i, f: (f, 0)),
        ],
        out_specs=pl.BlockSpec((None, tm, d), lambda b, i, f: (b, i, 0)),
        out_shape=jax.ShapeDtypeStruct((bsz, l, d), F32),
        scratch_shapes=[pltpu.VMEM((tm, d), BF16), pltpu.VMEM((tm, d), F32)],
        compiler_params=_cp(("parallel", "parallel", "arbitrary"), 48),
        name="ffn",
    )(x, g.reshape(1, d), sh, sc, gate, wg, wu, wd)


def _out_kernel(y_ref, w_ref, b_ref, x_ref, gate_ref, o_ref):
    o_ref[...] = x_ref[...] + gate_ref[...] * (_bdot(y_ref[...].astype(BF16), w_ref[...]) + b_ref[...])


def _out_proj(y, w, bias, x, gate, tm_pref=512, tn_pref=1024):
    bsz, l, d = x.shape
    k = w.shape[0]
    tm, tn = _tile(l, tm_pref), _tile(d, tn_pref)
    return pl.pallas_call(
        _out_kernel,
        grid=(bsz, l // tm, d // tn),
        in_specs=[
            pl.BlockSpec((None, tm, k), lambda b, i, j: (b, i, 0)),
            pl.BlockSpec((k, tn), lambda b, i, j: (0, j)),
            pl.BlockSpec((1, tn), lambda b, i, j: (0, j)),
            pl.BlockSpec((None, tm, tn), lambda b, i, j: (b, i, j)),
            pl.BlockSpec((None, 1, tn), lambda b, i, j: (b, 0, j)),
        ],
        out_specs=pl.BlockSpec((None, tm, tn), lambda b, i, j: (b, i, j)),
        out_shape=jax.ShapeDtypeStruct((bsz, l, d), F32),
        compiler_params=_cp(("parallel", "parallel", "parallel"), 40),
        name="out_proj",
    )(y, w, bias.reshape(1, d), x, gate)


def _final_norm_kernel(x_ref, g_ref, o_ref):
    x = x_ref[...]
    o_ref[...] = (x * lax.rsqrt(jnp.mean(x * x, axis=-1, keepdims=True) + NORM_EPS)) * g_ref[...]


def _final_norm(x, g, tm_pref=512):
    bsz, l, d = x.shape
    tm = _tile(l, tm_pref)
    return pl.pallas_call(
        _final_norm_kernel,
        grid=(bsz, l // tm),
        in_specs=[pl.BlockSpec((None, tm, d), lambda b, i: (b, i, 0)), pl.BlockSpec((1, d), lambda b, i: (0, 0))],
        out_specs=pl.BlockSpec((None, tm, d), lambda b, i: (b, i, 0)),
        out_shape=jax.ShapeDtypeStruct((bsz, l, d), F32),
        compiler_params=_cp(("parallel", "parallel")),
        name="final_norm",
    )(x, g.reshape(1, d))


def _conv3(x, prev_row, next_row, w):
    tm = x.shape[0]
    row = lax.broadcasted_iota(jnp.int32, x.shape, 0)
    up = jnp.where(row == 0, prev_row, pltpu.roll(x, 1, 0))
    dn = jnp.where(row == tm - 1, next_row, pltpu.roll(x, tm - 1, 0))
    return up * w[0:1] + x * w[1:2] + dn * w[2:3]


def _halo_rows(p_ref, n_ref):
    i = pl.program_id(1)
    last = pl.num_programs(1) - 1
    prev_row = jnp.where(i == 0, 0.0, p_ref[HALO - 1:HALO, :].astype(F32))
    next_row = jnp.where(i == last, 0.0, n_ref[0:1, :].astype(F32))
    return prev_row, next_row


def _hy_gate_kernel(m0, p0, n0, m1, p1, n1, m2, p2, n2, w0, w1, w2, b0, b1, b2, vg_ref, x0_ref):
    def comp(m, p, n, w, b):
        pr, nx = _halo_rows(p, n)
        return _conv3(m[...].astype(F32), pr, nx, w[...]) + b[...]

    u0 = comp(m0, p0, n0, w0, b0)
    u1 = comp(m1, p1, n1, w1, b1)
    u2 = comp(m2, p2, n2, w2, b2)
    vg_ref[...] = (u2 * u1).astype(vg_ref.dtype)
    x0_ref[...] = u0.astype(x0_ref.dtype)


def _sc_gate_kernel(m0, m1, p1, n1, m2, p2, n2, w_ref, o_ref):
    pr1, nx1 = _halo_rows(p1, n1)
    pr2, nx2 = _halo_rows(p2, n2)
    prod = m1[...].astype(F32) * m2[...].astype(F32)
    o_ref[...] = (m0[...].astype(F32) * _conv3(prod, pr1 * pr2, nx1 * nx2, w_ref[...])).astype(o_ref.dtype)


def _halo_specs(tm, tc, l, col):
    r = tm // HALO
    nblk = l // HALO
    return [
        pl.BlockSpec((None, tm, tc), lambda b, i, j: (b, i, col + j)),
        pl.BlockSpec((None, HALO, tc), lambda b, i, j: (b, jnp.maximum(i * r - 1, 0), col + j)),
        pl.BlockSpec((None, HALO, tc), lambda b, i, j: (b, jnp.minimum((i + 1) * r, nblk - 1), col + j)),
    ]


def _hy_gate(z, short_w, short_b, tc_pref=1024):
    bsz, l, d3 = z.shape
    d = d3 // 3
    tm = _tile(l, 256)
    tc = _tile(d, tc_pref)
    nd = d // tc
    in_specs = []
    for comp in range(3):
        in_specs += _halo_specs(tm, tc, l, comp * nd)
    for comp in range(3):
        in_specs.append(pl.BlockSpec((3, tc), lambda b, i, j, c=comp: (0, c * nd + j)))
    for comp in range(3):
        in_specs.append(pl.BlockSpec((1, tc), lambda b, i, j, c=comp: (0, c * nd + j)))
    ospec = pl.BlockSpec((None, tm, tc), lambda b, i, j: (b, i, j))
    return pl.pallas_call(
        _hy_gate_kernel,
        grid=(bsz, l // tm, nd),
        in_specs=in_specs,
        out_specs=[ospec, ospec],
        out_shape=[jax.ShapeDtypeStruct((bsz, l, d), BF16)] * 2,
        compiler_params=_cp(("parallel", "parallel", "parallel"), 40),
        name="hyena_gate",
    )(*([z] * 9), short_w, short_w, short_w, *([short_b.reshape(1, d3)] * 3))


def _sc_gate(z, conv_w, tc_pref=1024):
    bsz, l, d3 = z.shape
    d = d3 // 3
    tm = _tile(l, 256)
    tc = _tile(d, tc_pref)
    nd = d // tc
    in_specs = [_halo_specs(tm, tc, l, 0)[0]] + _halo_specs(tm, tc, l, nd) + _halo_specs(tm, tc, l, 2 * nd)
    in_specs.append(pl.BlockSpec((3, tc), lambda b, i, j: (0, j)))
    return pl.pallas_call(
        _sc_gate_kernel,
        grid=(bsz, l // tm, nd),
        in_specs=in_specs,
        out_specs=pl.BlockSpec((None, tm, tc), lambda b, i, j: (b, i, j)),
        out_shape=jax.ShapeDtypeStruct((bsz, l, d), BF16),
        compiler_params=_cp(("parallel", "parallel", "parallel"), 40),
        name="shortconv_gate",
    )(*([z] * 7), conv_w)


def _filter_features(l):
    emb = 33
    bands = (emb - 1) // 2
    rows = jnp.arange(2 * l, dtype=jnp.int32)
    pos = jnp.where(rows < l, rows, 2 * l - rows).astype(F32)[:, None]
    t = pos / (l - 1)
    omega = 2.0 * math.pi * pos / l
    fb = jnp.linspace(1e-4, bands - 1, bands, dtype=F32)[None, :]
    z = jnp.concatenate([t, jnp.cos(fb * omega), -jnp.sin(fb * omega)], axis=-1)
    return jnp.pad(z, ((0, 0), (0, LANES - emb)))


def _filt_kernel(z_ref, w1, b1, w2, b2, w3, b3, fr, w4h, w4l, dab, o_ref):
    z = z_ref[...]

    def hdot(a, b):
        return jnp.dot(a, b, precision=HIGHEST, preferred_element_type=F32)

    f = fr[...]
    h = jnp.sin(f * (hdot(z, w1[...]) + b1[...]))
    h = jnp.sin(f * (hdot(h, w2[...]) + b2[...]))
    h = jnp.sin(f * (hdot(h, w3[...]) + b3[...]))
    h_hi = h.astype(BF16)
    h_lo = (h - h_hi.astype(F32)).astype(BF16)
    hw = _bdot(h_hi, w4h[...]) + _bdot(h_lo, w4h[...]) + _bdot(h_hi, w4l[...])
    k = hw * jnp.exp(-(z[:, 0:1] * dab[...]))
    row = lax.broadcasted_iota(jnp.int32, k.shape, 0)
    first_bwd = pl.program_id(0) * 2 == pl.num_programs(0)
    o_ref[...] = jnp.where((row == 0) & first_bwd, 0.0, k).astype(o_ref.dtype)


def _hyena_filter(l, p, d, tc_pref=2048):
    w1, b1, w2, b2, w3, b3, w4, freq = p
    order = w2.shape[0]
    zf = _filter_features(l)

    def pad2(a, rows, cols):
        return jnp.pad(a, ((0, rows - a.shape[0]), (0, cols - a.shape[1])))

    w1p = pad2(w1, LANES, LANES)
    w2p = pad2(w2, LANES, LANES)
    w3p = pad2(w3, LANES, LANES)
    b1p = pad2(b1.reshape(1, order), 1, LANES)
    b2p = pad2(b2.reshape(1, order), 1, LANES)
    b3p = pad2(b3.reshape(1, order), 1, LANES)
    frp = pad2(freq.reshape(1, order), 1, LANES)
    w4p = pad2(w4, LANES, 2 * d)
    w4h = w4p.astype(BF16)
    w4l = (w4p - w4h.astype(F32)).astype(BF16)
    max_decay = math.log(HYENA_TARGET) / HYENA_FAST_DECAY
    min_decay = math.log(HYENA_TARGET) / HYENA_SLOW_DECAY
    dab = jnp.abs(jnp.linspace(min_decay, max_decay, d, dtype=F32)).reshape(1, d)
    tc = _tile(d, tc_pref)
    nd = d // tc
    r = _tile(l, 512)
    nb = 2 * l // r
    small = pl.BlockSpec((LANES, LANES), lambda i, j: (0, 0))
    vec = pl.BlockSpec((1, LANES), lambda i, j: (0, 0))
    bank = pl.BlockSpec((LANES, tc), lambda i, j: (0, (i * 2 // nb) * nd + j))
    return pl.pallas_call(
        _filt_kernel,
        grid=(nb, nd),
        in_specs=[pl.BlockSpec((r, LANES), lambda i, j: (i, 0)),
                  small, vec, small, vec, small, vec, vec, bank, bank, pl.BlockSpec((1, tc), lambda i, j: (0, j))],
        out_specs=pl.BlockSpec((r, tc), lambda i, j: (i, j)),
        out_shape=jax.ShapeDtypeStruct((2 * l, d), BF16),
        compiler_params=_cp(("parallel", "parallel"), 40),
        name="hyena_filter",
    )(zf, w1p, b1p, w2p, b2p, w3p, b3p, frp, w4h, w4l, dab)


def _cs(m, n):
    ang = (m % n).astype(F32) * (2.0 * math.pi / n)
    return jnp.cos(ang), jnp.sin(ang)


def _blk(a, b, c, d):
    return jnp.concatenate([jnp.concatenate([a, b], axis=-1), jnp.concatenate([c, d], axis=-1)], axis=-2)


def _fft_tables(l):
    n = 2 * l
    n1 = n // FFT_N2
    a_cnt = n1 // 2
    k1 = jnp.arange(n1, dtype=jnp.int32)
    b = jnp.arange(FFT_N2, dtype=jnp.int32)
    a_full = jnp.arange(n1, dtype=jnp.int32)
    pos = FFT_N2 * a_full[None, :] + b[:, None]
    c, s = _cs(k1[None, :, None] * pos[:, None, :], n)
    cd, sd = c[:, :, :a_cnt], s[:, :, :a_cnt]
    g_data = _blk(cd, sd, -sd, cd).astype(BF16)
    g_filt = jnp.concatenate([c, -s], axis=-2).astype(BF16)
    ct, st = jnp.swapaxes(cd, 1, 2), jnp.swapaxes(sd, 1, 2)
    g_fin = _blk(ct, -st, st, ct).astype(BF16)
    c2, s2 = _cs(b[:, None] * b[None, :], FFT_N2)
    f2_fwd = _blk(c2, s2, -s2, c2).astype(BF16)
    f2_inv = _blk(c2, -s2, s2, c2).astype(BF16)
    return g_data, g_filt, g_fin, f2_fwd, f2_inv


def _per_b_matmul(x, g_ref):
    xt = pltpu.einshape("rjc->jrc", x)
    return jnp.stack([_bdot(g_ref[j], xt[j]) for j in range(HALO)], axis=0)


def _s1_kernel(x_ref, g_ref, o_ref):
    o_ref[...] = pltpu.einshape("jmc->mjc", _per_b_matmul(x_ref[...], g_ref).astype(BF16))


def _fft_stage1(x4, g, tc_pref=512):
    r, nt, sub, d = x4.shape
    m = g.shape[1]
    tc = _tile(d, tc_pref)
    return pl.pallas_call(
        _s1_kernel,
        grid=(nt, d // tc),
        in_specs=[pl.BlockSpec((r, None, sub, tc), lambda t, j: (0, t, 0, j)),
                  pl.BlockSpec((sub, m, r), lambda t, j: (t, 0, 0))],
        out_specs=pl.BlockSpec((m, None, sub, tc), lambda t, j: (0, t, 0, j)),
        out_shape=jax.ShapeDtypeStruct((m, nt, sub, d), BF16),
        compiler_params=_cp(("parallel", "arbitrary"), 48),
        name="fft_stage1",
    )(x4, g)


def _dmid_kernel(y_ref, yk_ref, ff_ref, fi_ref, t_ref, *, scale):
    n2 = y_ref.shape[1]
    ff = ff_ref[...]
    s = _bdot(ff, jnp.concatenate([y_ref[0], y_ref[1]], axis=0))
    ks = _bdot(ff, jnp.concatenate([yk_ref[0], yk_ref[1]], axis=0)) * scale
    sr, si = s[:n2], s[n2:]
    kr, ki = ks[:n2], ks[n2:]
    p = jnp.concatenate([sr * kr - si * ki, sr * ki + si * kr], axis=0).astype(BF16)
    t = _bdot(fi_ref[...], p)
    t_ref[0] = t[:n2].astype(t_ref.dtype)
    t_ref[1] = t[n2:].astype(t_ref.dtype)


def _fft_data_mid(y1, yk1, f2_fwd, f2_inv, scale, tc_pref=1024):
    m, n2, d = y1.shape
    n1 = m // 2
    tc = _tile(d, tc_pref)
    tbl = pl.BlockSpec((2 * n2, 2 * n2), lambda k, j: (0, 0))
    yspec = pl.BlockSpec((2, None, n2, tc), lambda k, j: (0, k, 0, j))
    return pl.pallas_call(
        functools.partial(_dmid_kernel, scale=scale),
        grid=(n1, d // tc),
        in_specs=[yspec, yspec, tbl, tbl],
        out_specs=yspec,
        out_shape=jax.ShapeDtypeStruct((2, n1, n2, d), BF16),
        compiler_params=_cp(("parallel", "parallel"), 48),
        name="fft_data_mid",
    )(y1.reshape(2, n1, n2, d), yk1.reshape(2, n1, n2, d), f2_fwd, f2_inv)


def _fin_kernel(t_ref, g_ref, vg_ref, x0_ref, skip_ref, o_ref):
    conv = pltpu.einshape("jmc->mjc", _per_b_matmul(t_ref[...], g_ref))
    y = (conv + vg_ref[...].astype(F32) * skip_ref[...]) * x0_ref[...].astype(F32)
    o_ref[...] = y.astype(o_ref.dtype)


def _fft_final(t4, g_fin, vg, x0, skip, tc_pref=512):
    _, n1, n2, d = t4.shape
    bsz, l, _ = vg.shape
    nt = n2 // HALO
    rows_in = 2 * n1
    rows_out = bsz * l // n2
    tc = _tile(d, tc_pref)
    vspec = pl.BlockSpec((rows_out, None, HALO, tc), lambda t, j: (0, t, 0, j))
    out = pl.pallas_call(
        _fin_kernel,
        grid=(nt, d // tc),
        in_specs=[pl.BlockSpec((rows_in, None, HALO, tc), lambda t, j: (0, t, 0, j)),
                  pl.BlockSpec((HALO, rows_out, rows_in), lambda t, j: (t, 0, 0)),
                  vspec, vspec, pl.BlockSpec((1, 1, tc), lambda t, j: (0, 0, j))],
        out_specs=vspec,
        out_shape=jax.ShapeDtypeStruct((rows_out, nt, HALO, d), BF16),
        compiler_params=_cp(("parallel", "arbitrary"), 48),
        name="fft_final",
    )(t4.reshape(rows_in, nt, HALO, d), g_fin, vg.reshape(rows_out, nt, HALO, d),
      x0.reshape(rows_out, nt, HALO, d), skip.reshape(1, 1, d))
    return out.reshape(bsz, l, d)


def _hyena_longconv_main(z, hp, tables):
    (short_w, short_b, fparams, skip) = hp
    g_data, g_filt, g_fin, f2_fwd, f2_inv = tables
    bsz, l, d3 = z.shape
    d = d3 // 3
    n = 2 * l
    n1 = n // FFT_N2
    nt = FFT_N2 // HALO
    vg, x0 = _hy_gate(z, short_w, short_b)
    kt = _hyena_filter(l, fparams, d)
    yk1 = _fft_stage1(kt.reshape(n1, nt, HALO, d), g_filt)
    y1 = _fft_stage1(vg.reshape(bsz * l // FFT_N2, nt, HALO, d), g_data)
    t4 = _fft_data_mid(y1.reshape(2 * n1, FFT_N2, d), yk1.reshape(2 * n1, FFT_N2, d), f2_fwd, f2_inv, 1.0 / n)
    return _fft_final(t4, g_fin, vg, x0, skip)


def _ctxconv_kernel(vg_ref, x0_ref, k_ref, skip_ref, fd_ref, ff_ref, fi_ref, o_ref, *, scale):
    lc = vg_ref.shape[1]
    n = 2 * lc
    z = jnp.concatenate([vg_ref[0], vg_ref[1]], axis=0)
    s = _bdot(fd_ref[...], z)
    ks = _bdot(ff_ref[...], k_ref[...]) * scale
    sr, si = s[:n], s[n:]
    kr, ki = ks[:n], ks[n:]
    p = jnp.concatenate([sr * kr - si * ki, sr * ki + si * kr], axis=0).astype(BF16)
    y = _bdot(fi_ref[...], p)
    sk = skip_ref[...]
    o_ref[0] = ((y[:lc] + vg_ref[0].astype(F32) * sk) * x0_ref[0].astype(F32)).astype(o_ref.dtype)
    o_ref[1] = ((y[lc:] + vg_ref[1].astype(F32) * sk) * x0_ref[1].astype(F32)).astype(o_ref.dtype)


def _hyena_longconv_ctx(z, hp, tc_pref=512):
    (short_w, short_b, fparams, skip) = hp
    bsz, lc, d3 = z.shape
    d = d3 // 3
    n = 2 * lc
    vg, x0 = _hy_gate(z, short_w, short_b)
    kt = _hyena_filter(lc, fparams, d)
    kf = jnp.arange(n, dtype=jnp.int32)
    c, s = _cs(kf[:, None] * kf[None, :], n)
    cd, sd = c[:, :lc], s[:, :lc]
    fd = _blk(cd, sd, -sd, cd).astype(BF16)
    ff = jnp.concatenate([c, -s], axis=0).astype(BF16)
    fi = _blk(cd.T, -sd.T, sd.T, cd.T).astype(BF16)
    tc = _tile(d, tc_pref)
    xspec = pl.BlockSpec((bsz, lc, tc), lambda j: (0, 0, j))
    full = lambda a: pl.BlockSpec(a.shape, lambda j: (0, 0))
    return pl.pallas_call(
        functools.partial(_ctxconv_kernel, scale=1.0 / n),
        grid=(d // tc,),
        in_specs=[xspec, xspec, pl.BlockSpec((n, tc), lambda j: (0, j)), pl.BlockSpec((1, tc), lambda j: (0, j)),
                  full(fd), full(ff), full(fi)],
        out_specs=xspec,
        out_shape=jax.ShapeDtypeStruct((bsz, lc, d), BF16),
        compiler_params=_cp(("parallel",), 40),
        name="hyena_ctx_conv",
    )(vg, x0, kt, skip.reshape(1, d), fd, ff, fi)


def _rope_tables(l, dh):
    rows = l // GRID_W
    d_axis = dh // 2
    inv_freq = ROPE_THETA ** (-jnp.arange(0, d_axis, 2, dtype=F32) / d_axis)
    ang_r = jnp.arange(rows, dtype=F32)[:, None] * inv_freq
    ang_c = jnp.arange(GRID_W, dtype=F32)[:, None] * inv_freq
    half = d_axis // 2
    ang = jnp.concatenate([jnp.broadcast_to(ang_r[:, None, :], (rows, GRID_W, half)),
                           jnp.broadcast_to(ang_c[None, :, :], (rows, GRID_W, half))], axis=-1)
    ang = ang.reshape(rows * GRID_W, d_axis)
    cos, sin = jnp.cos(ang), jnp.sin(ang)
    cosf = jnp.repeat(cos, 2, axis=-1)
    sinf = jnp.stack([-sin, sin], axis=-1).reshape(rows * GRID_W, dh)
    return cosf, sinf


def _prep_kernel(x_ref, cos_ref, sin_ref, qg_ref, kg_ref, *outs, nq, nkv, dh, qscale):
    cosf = cos_ref[...]
    sinf = sin_ref[...]
    even = (lax.broadcasted_iota(jnp.int32, cosf.shape, 1) % 2) == 0

    def norm_rope(xh, g):
        y = (xh * lax.rsqrt(jnp.mean(xh * xh, axis=-1, keepdims=True) + NORM_EPS)) * g
        sw = jnp.where(even, pltpu.roll(y, dh - 1, 1), pltpu.roll(y, 1, 1))
        return y * cosf + sw * sinf

    if nq:
        q_ref, k_ref, vt_ref = outs
        for h in range(nq):
            xh = x_ref[:, h * dh:(h + 1) * dh].astype(F32)
            q_ref[:, h * dh:(h + 1) * dh] = (norm_rope(xh, qg_ref[...]) * qscale).astype(BF16)
    else:
        k_ref, vt_ref = outs
    for j in range(nkv):
        c0 = (nq + j) * dh
        k_ref[j] = norm_rope(x_ref[:, c0:c0 + dh].astype(F32), kg_ref[...]).astype(BF16)
    for j in range(nkv):
        c0 = (nq + nkv + j) * dh
        vt_ref[j] = x_ref[:, c0:c0 + dh].astype(F32).T.astype(BF16)


def _attn_prep(x, cosf, sinf, q_g, k_g, nq, nkv, dh, tm_pref=256):
    bsz, l, cols = x.shape
    tm = _tile(l, tm_pref)
    out_specs = [pl.BlockSpec((None, nkv, tm, dh), lambda b, i: (b, 0, i, 0)),
                 pl.BlockSpec((None, nkv, dh, tm), lambda b, i: (b, 0, 0, i))]
    out_shape = [jax.ShapeDtypeStruct((bsz, nkv, l, dh), BF16), jax.ShapeDtypeStruct((bsz, nkv, dh, l), BF16)]
    if nq:
        out_specs = [pl.BlockSpec((None, tm, nq * dh), lambda b, i: (b, i, 0))] + out_specs
        out_shape = [jax.ShapeDtypeStruct((bsz, l, nq * dh), BF16)] + out_shape
    tab = pl.BlockSpec((tm, dh), lambda b, i: (i, 0))
    gain = pl.BlockSpec((1, dh), lambda b, i: (0, 0))
    return pl.pallas_call(
        functools.partial(_prep_kernel, nq=nq, nkv=nkv, dh=dh, qscale=dh ** -0.5 * math.log2(math.e)),
        grid=(bsz, l // tm),
        in_specs=[pl.BlockSpec((None, tm, cols), lambda b, i: (b, i, 0)), tab, tab, gain, gain],
        out_specs=out_specs,
        out_shape=out_shape,
        compiler_params=_cp(("parallel", "parallel"), 40),
        name="attn_prep",
    )(x, cosf, sinf, q_g.reshape(1, dh), k_g.reshape(1, dh))


def _flash_kernel(q_ref, k_ref, vt_ref, o_ref, sa_scr, sb_scr, ca_scr, cb_scr, m_scr, l_scr, acc_scr, *, group, dh):
    kk = pl.program_id(3)
    nk = pl.num_programs(3) - 1

    def score_phase(s_w, c_w):
        k = k_ref[...]
        for h in range(group):
            s = lax.dot_general(k, q_ref[:, h * dh:(h + 1) * dh], (((1,), (1,)), ((), ())),
                                preferred_element_type=F32)
            s_w[h] = s
            c_w[h] = jnp.max(s, axis=0, keepdims=True)

    def softmax_phase(s_r, c_r):
        vt = vt_ref[...]
        for h in range(group):
            m_old = m_scr[h]
            m_new = jnp.maximum(m_old, c_r[h])
            alpha = jnp.exp2(m_old - m_new)
            p = jnp.exp2(s_r[h] - m_new)
            l_scr[h] = alpha * l_scr[h] + jnp.sum(p, axis=0, keepdims=True)
            m_scr[h] = m_new
            acc_scr[h] = alpha * acc_scr[h] + _bdot(vt, p.astype(BF16))

    @pl.when(kk == 0)
    def _():
        m_scr[...] = jnp.full_like(m_scr, -jnp.inf)
        l_scr[...] = jnp.zeros_like(l_scr)
        acc_scr[...] = jnp.zeros_like(acc_scr)
        score_phase(sa_scr, ca_scr)

    middle = (kk > 0) & (kk < nk)

    @pl.when(middle & (kk % 2 == 1))
    def _():
        score_phase(sb_scr, cb_scr)
        softmax_phase(sa_scr, ca_scr)

    @pl.when(middle & (kk % 2 == 0))
    def _():
        score_phase(sa_scr, ca_scr)
        softmax_phase(sb_scr, cb_scr)

    def drain(s_r, c_r):
        softmax_phase(s_r, c_r)
        for h in range(group):
            o = acc_scr[h] / l_scr[h]
            o_ref[:, h * dh:(h + 1) * dh] = o.T.astype(o_ref.dtype)

    @pl.when((kk == nk) & (kk % 2 == 1))
    def _():
        drain(sa_scr, ca_scr)

    @pl.when((kk == nk) & (kk % 2 == 0))
    def _():
        drain(sb_scr, cb_scr)


def _key_tile(lk, pref):
    best = LANES
    t = LANES
    while t <= min(lk, pref):
        if lk % t == 0:
            best = t
        t += LANES
    return best


def _flash_attention(q, k, vt, group, dh, tq_pref=512, tk_pref=1280):
    bsz, l, hd = q.shape
    nkv, lk = k.shape[1], k.shape[2]
    tq = _tile(l, tq_pref)
    tk = _key_tile(lk, tk_pref)
    gw = group * dh
    nk = lk // tk
    stat = pltpu.VMEM((group, 1, tq), F32)
    score = pltpu.VMEM((group, tk, tq), F32)
    return pl.pallas_call(
        functools.partial(_flash_kernel, group=group, dh=dh),
        grid=(bsz, nkv, l // tq, nk + 1),
        in_specs=[
            pl.BlockSpec((None, tq, gw), lambda b, g, i, kk: (b, i, g)),
            pl.BlockSpec((None, None, tk, dh), lambda b, g, i, kk: (b, g, jnp.minimum(kk, nk - 1), 0)),
            pl.BlockSpec((None, None, dh, tk), lambda b, g, i, kk: (b, g, 0, jnp.maximum(kk - 1, 0))),
        ],
        out_specs=pl.BlockSpec((None, tq, gw), lambda b, g, i, kk: (b, i, g)),
        out_shape=jax.ShapeDtypeStruct((bsz, l, hd), BF16),
        scratch_shapes=[score, score, stat, stat, stat, stat, pltpu.VMEM((group, dh, tq), F32)],
        compiler_params=_cp(("parallel", "parallel", "parallel", "arbitrary"), 56),
        name="flash_attention",
    )(q, k, vt)


def kernel(x, c, ctx, c_ctx, norm1_g, norm2_g, w_mod, b_mod, ffn_w_gate, ffn_w_up, ffn_w_down, hy_w_in, hy_b_in,
           hy_short_w, hy_short_b, hy_f_w1, hy_f_b1, hy_f_w2, hy_f_b2, hy_f_w3, hy_f_b3, hy_f_w4, hy_f_freq,
           hy_skip, hy_w_out, hy_b_out, sc_w_in, sc_conv_w, sc_w_out, at_w_qkv, at_q_g, at_k_g, at_w_o, final_g):
    bsz, l, d = x.shape
    lc = ctx.shape[1]
    depth = norm1_g.shape[0]
    dh = at_q_g.shape[-1]
    n_heads = d // dh
    n_kv = (at_w_qkv.shape[-1] - d) // (2 * dh)
    group = n_heads // n_kv
    assert bsz == 2, "the long convolution packs exactly two batch samples into one complex signal"
    assert l % FFT_N2 == 0 and l % GRID_W == 0

    attn_layers = [i for i in range(depth) if i % N_MIXERS == 2]
    last_ctx_reader = attn_layers[-1] if attn_layers else -1

    cond8 = jnp.zeros((SUBLANES, d), F32).at[:bsz].set(c).at[bsz].set(c_ctx)
    mods = _adaln_all(cond8, w_mod, b_mod)

    def mod_set(i, ctx_rows):
        m = mods[i]
        if ctx_rows:
            rows = jnp.broadcast_to(m[bsz:bsz + 1], (bsz, N_MOD * d))
        else:
            rows = m[:bsz]
        return [rows[:, None, k * d:(k + 1) * d] for k in range(N_MOD)]

    zeros = lambda n: jnp.zeros((n,), F32)
    bf = lambda w: w.astype(BF16)
    tables = None
    if any(i % N_MIXERS == 0 for i in range(depth)):
        tables = _fft_tables(l)

    xc = ctx
    for i in range(depth):
        kind, j = i % N_MIXERS, i // N_MIXERS
        ctx_in = i <= last_ctx_reader
        ctx_out = i < last_ctx_reader
        sh1, sc1, g1, sh2, sc2, g2 = mod_set(i, False)
        if ctx_in:
            csh1, csc1, cg1, csh2, csc2, cg2 = mod_set(i, True)
        wg, wu, wd = bf(ffn_w_gate[i]), bf(ffn_w_up[i]), bf(ffn_w_down[i])
        if kind == 0:
            w_in, w_out = bf(hy_w_in[j]), bf(hy_w_out[j])
            hp = (hy_short_w[j], hy_short_b[j],
                  (hy_f_w1[j], hy_f_b1[j], hy_f_w2[j], hy_f_b2[j], hy_f_w3[j], hy_f_b3[j], hy_f_w4[j], hy_f_freq[j]),
                  hy_skip[j])
            z = _norm_mod_matmul(x, norm1_g[i], sh1, sc1, w_in, hy_b_in[j])
            y = _hyena_longconv_main(z, hp, tables)
            x = _out_proj(y, w_out, hy_b_out[j], x, g1)
            if ctx_out:
                zc = _norm_mod_matmul(xc, norm1_g[i], csh1, csc1, w_in, hy_b_in[j])
                yc = _hyena_longconv_ctx(zc, hp)
                xc = _out_proj(yc, w_out, hy_b_out[j], xc, cg1)
        elif kind == 1:
            w_in, w_out = bf(sc_w_in[j]), bf(sc_w_out[j])
            z = _norm_mod_matmul(x, norm1_g[i], sh1, sc1, w_in, zeros(3 * d))
            x = _out_proj(_sc_gate(z, sc_conv_w[j]), w_out, zeros(d), x, g1)
            if ctx_out:
                zc = _norm_mod_matmul(xc, norm1_g[i], csh1, csc1, w_in, zeros(3 * d))
                xc = _out_proj(_sc_gate(zc, sc_conv_w[j]), w_out, zeros(d), xc, cg1)
        else:
            w_qkv, w_o = bf(at_w_qkv[j]), bf(at_w_o[j])
            qkv_cols = w_qkv.shape[1]
            qkv = _norm_mod_matmul(x, norm1_g[i], sh1, sc1, w_qkv, zeros(qkv_cols))
            cosf, sinf = _rope_tables(l, dh)
            q, k, vt = _attn_prep(qkv, cosf, sinf, at_q_g[j], at_k_g[j], n_heads, n_kv, dh)
            if ctx_out:
                qkv_c = _norm_mod_matmul(xc, norm1_g[i], csh1, csc1, w_qkv, zeros(qkv_cols))
                qc, kc, vtc = _attn_prep(qkv_c, jnp.ones((lc, dh), F32), jnp.zeros((lc, dh), F32),
                                         at_q_g[j], at_k_g[j], n_heads, n_kv, dh)
                oc = _flash_attention(qc, kc, vtc, group, dh)
                xc_next = _out_proj(oc, w_o, zeros(d), xc, cg1)
            else:
                kv_c = _norm_mod_matmul(xc, norm1_g[i], csh1, csc1, w_qkv[:, n_heads * dh:], zeros(qkv_cols - d))
                kc, vtc = _attn_prep(kv_c, jnp.ones((lc, dh), F32), jnp.zeros((lc, dh), F32),
                                     at_q_g[j], at_k_g[j], 0, n_kv, dh)
            k_all = jnp.concatenate([k, kc], axis=2)
            vt_all = jnp.concatenate([vt, vtc], axis=3)
            o = _flash_attention(q, k_all, vt_all, group, dh)
            x = _out_proj(o, w_o, zeros(d), x, g1)
            if ctx_out:
                xc = xc_next
        x = _ffn(x, norm2_g[i], sh2, sc2, g2, wg, wu, wd)
        if ctx_out:
            xc = _ffn(xc, norm2_g[i], csh2, csc2, cg2, wg, wu, wd)
    return _final_norm(x, final_g)
```
